```python
import math, functools
import jax, jax.numpy as jnp
from jax import lax
import numpy as np

D_MODEL = 2048
BATCH = 4
SEQ = 2048
DEPTH = 4
DEC_BATCH = 128
DEC_SEQ = 4
PAST_LEN = 8192
PAGE_SIZE = 128

H_A = 8
HD_A = 128
KV_A = 2
H_I = 16
D_I = 64
TOPK_MAX = 256
N_BUCKETS = 32
MAX_DISTANCE = 128
H_B = 8
Q_LORA = 512
KV_LORA = 256
NOPE_B = 128
ROPE_B = 64
V_B = 128
ROPE_BASE = 10000.0
H_C = 8
DK_C = D_MODEL // H_C
DV_C = 2 * D_MODEL // H_C
RET_CHUNK = 128
N_GROUPS = 4
E_PER_GROUP = 8
N_EXPERTS = N_GROUPS * E_PER_GROUP
D_EXPERT = 512
TOP_K_EXPERTS = 2
QBLK = 128
EPS = 1e-6

EVEN_SPLITS = (H_A * HD_A, KV_A * HD_A, KV_A * HD_A, H_I * D_I, D_I, H_I, Q_LORA, KV_LORA, ROPE_B)
ODD_SPLITS = (H_C * DK_C, H_C * DK_C, H_C * DV_C, H_C * DV_C)
EVEN_IN = sum(EVEN_SPLITS)
ODD_IN = sum(ODD_SPLITS)
EVEN_OUT = H_A * HD_A + H_B * V_B
ODD_OUT = H_C * DV_C

kernel_name = 'hybrid_dsa_mla_retention_hmoe_step'

F32 = jnp.float32


def rmsnorm(x, g):
    xf = x.astype(F32)
    xf = xf * lax.rsqrt(jnp.mean(xf * xf, axis=-1, keepdims=True) + EPS)
    return xf.astype(x.dtype) * g


def rope(x, pos):
    half = x.shape[-1] // 2
    inv = ROPE_BASE ** (-jnp.arange(half, dtype=F32) / half)
    ang = pos.astype(F32)[:, None] * inv[None, :]
    ang = ang.reshape((1, ang.shape[0]) + (1,) * (x.ndim - 3) + (half,))
    cos, sin = jnp.cos(ang), jnp.sin(ang)
    xf = x.astype(F32)
    x1, x2 = xf[..., :half], xf[..., half:]
    return jnp.concatenate([x1 * cos - x2 * sin, x1 * sin + x2 * cos], axis=-1).astype(x.dtype)


def t5_bucket(dist):
    max_exact = N_BUCKETS // 2
    n = jnp.maximum(dist, 0)
    large = max_exact + (jnp.log(jnp.maximum(n, 1).astype(F32) / max_exact)
                         / math.log(MAX_DISTANCE / max_exact) * (N_BUCKETS - max_exact)).astype(jnp.int32)
    large = jnp.minimum(large, N_BUCKETS - 1)
    return jnp.where(n < max_exact, n, large)


def ada_params(c, w, b):
    m = jax.nn.silu(c) @ w + b
    return m.reshape(c.shape[0], 6, 1, D_MODEL)


def modulate(h, shift, scale):
    return h * (1 + scale) + shift


def gather_rows(rows, idx):
    return jax.vmap(lambda r, i: r[i])(rows, idx)


def query_blocks(fn, qs, qpos):
    nq = qpos.shape[0]
    if nq <= QBLK:
        return fn(qs, qpos)
    nb = nq // QBLK
    qs_b = tuple(a.reshape((a.shape[0], nb, QBLK) + a.shape[2:]).swapaxes(0, 1) for a in qs)
    out = lax.map(lambda args: fn(args[0], args[1]), (qs_b, qpos.reshape(nb, QBLK)))
    out = out.swapaxes(0, 1)
    return out.reshape((out.shape[0], nq) + out.shape[3:])


def dsa_select(iq, iw, ik, qp, topk):
    s = jnp.einsum('bqhd,bld->bqhl', iq, ik).astype(F32)
    score = jnp.einsum('bqh,bqhl->bql', iw.astype(F32), jax.nn.relu(s) * D_I ** -0.5)
    valid = jnp.arange(ik.shape[1])[None, :] <= qp[:, None]
    score = jnp.where(valid[None], score, -jnp.inf)
    _, sel = lax.top_k(score, topk)
    return sel


def dsa_attend(q, k_sel, v_sel, sel, qp, rel_bias):
    b, nq = q.shape[:2]
    grp = H_A // KV_A
    qg = q.reshape(b, nq, KV_A, grp, HD_A)
    s = jnp.einsum('bqkgd,bqskd->bqkgs', qg, k_sel).astype(F32).reshape(b, nq, H_A, -1) * HD_A ** -0.5
    dist = qp[None, :, None] - sel
    bias = rel_bias[t5_bucket(dist)].astype(F32)
    s = s + jnp.moveaxis(bias, -1, 2)
    s = jnp.where((dist >= 0)[:, :, None, :], s, -jnp.inf)
    p = jax.nn.softmax(s, axis=-1).astype(v_sel.dtype).reshape(b, nq, KV_A, grp, -1)
    o = jnp.einsum('bqkgs,bqskd->bqkgd', p, v_sel)
    return o.reshape(b, nq, H_A * HD_A)


def mla_attend(q_lat, q_pe, ckv, kpe, qp, w_uv):
    b, nq = q_lat.shape[:2]
    s = (jnp.einsum('bqhc,blc->bhql', q_lat, ckv)
         + jnp.einsum('bqhr,blr->bhql', q_pe, kpe)).astype(F32) * (NOPE_B + ROPE_B) ** -0.5
    valid = jnp.arange(ckv.shape[1])[None, :] <= qp[:, None]
    p = jax.nn.softmax(jnp.where(valid, s, -jnp.inf), axis=-1).astype(ckv.dtype)
    o = jnp.einsum('bhql,blc->bqhc', p, ckv)
    return jnp.einsum('bqhc,chv->bqhv', o, w_uv).reshape(b, nq, H_B * V_B)


def even_project(h, pos, w_in, q_norm, kv_norm, w_uq, w_uk):
    b, t, _ = h.shape
    qa, ka, va, iq, ik, iw, cq, ckv, kpe = jnp.split(h @ w_in, np.cumsum(EVEN_SPLITS)[:-1].tolist(), axis=-1)
    qa = qa.reshape(b, t, H_A, HD_A)
    ka = ka.reshape(b, t, KV_A, HD_A)
    va = va.reshape(b, t, KV_A, HD_A)
    iq = iq.reshape(b, t, H_I, D_I)
    iw = iw * H_I ** -0.5
    qb = (rmsnorm(cq, q_norm) @ w_uq).reshape(b, t, H_B, NOPE_B + ROPE_B)
    q_lat = jnp.einsum('bthn,chn->bthc', qb[..., :NOPE_B], w_uk)
    q_pe = rope(qb[..., NOPE_B:], pos)
    ckv = rmsnorm(ckv, kv_norm)
    kpe = rope(kpe, pos)
    return qa, ka, va, iq, ik, iw, q_lat, q_pe, ckv, kpe


def even_mix(qs, qpos, ik_all, ckv_all, kpe_all, fetch, topk, rel_bias, w_uv, w_out):
    def block(qb, qp):
        qa, iq, iw, q_lat, q_pe = qb
        sel = dsa_select(iq, iw, ik_all, qp, topk)
        k_sel, v_sel = fetch(sel)
        oa = dsa_attend(qa, k_sel, v_sel, sel, qp, rel_bias)
        ob = mla_attend(q_lat, q_pe, ckv_all, kpe_all, qp, w_uv)
        return jnp.concatenate([oa, ob], axis=-1)
    return query_blocks(block, qs, qpos) @ w_out


def prompt_fetch(k_rows, v_rows, sel):
    return gather_rows(k_rows, sel), gather_rows(v_rows, sel)


def paged_fetch(cache_k, cache_v, layer, page_table, k_new, v_new, sel):
    is_past = (sel < PAST_LEN)[..., None, None]
    sp = jnp.minimum(sel, PAST_LEN - 1)
    phys = page_table[jnp.arange(sel.shape[0])[:, None, None], sp // PAGE_SIZE]
    off = sp % PAGE_SIZE
    sn = jnp.clip(sel - PAST_LEN, 0, k_new.shape[1] - 1)
    k_sel = jnp.where(is_past, cache_k[layer, phys, off], gather_rows(k_new, sn))
    v_sel = jnp.where(is_past, cache_v[layer, phys, off], gather_rows(v_new, sn))
    return k_sel, v_sel


def paged_rows(cache, layer, page_table):
    rows = cache[layer, page_table]
    return rows.reshape((rows.shape[0], rows.shape[1] * rows.shape[2]) + rows.shape[3:])


def retention_scan(q, k, v, s0):
    b, t, nh, _ = q.shape
    c = min(RET_CHUNK, t)
    nc = t // c
    log_g = jnp.log1p(-(2.0 ** (-5.0 - jnp.arange(nh, dtype=F32))))
    i = jnp.arange(c, dtype=F32)
    diff = i[:, None] - i[None, :]
    dmask = jnp.where(diff >= 0, jnp.exp(jnp.maximum(diff, 0.0)[None] * log_g[:, None, None]), 0.0)
    cross = jnp.exp((i[:, None] + 1.0) * log_g[None, :])
    kdec = jnp.exp((c - 1.0 - i)[:, None] * log_g[None, :])
    chunk_dec = jnp.exp(c * log_g)

    def to_chunks(a):
        return a.astype(F32).reshape((b, nc, c) + a.shape[2:]).swapaxes(0, 1)

    def step(s, inp):
        qc, kc, vc = inp
        a = jnp.einsum('bihd,bjhd->bhij', qc, kc) * dmask
        o = (jnp.einsum('bhij,bjhv->bihv', a, vc)
             + jnp.einsum('bihd,bhdv->bihv', qc, s) * cross[None, :, :, None])
        s = s * chunk_dec[None, :, None, None] + jnp.einsum('bjhd,bjhv->bhdv', kc * kdec[None, :, :, None], vc)
        return s, o

    s, o = lax.scan(step, s0.astype(F32), (to_chunks(q), to_chunks(k), to_chunks(v)))
    return o.swapaxes(0, 1).reshape(b, t, nh, -1), s


def retention_mixer(h, pos, s0, w_in, gn_gain, w_out):
    b, t, _ = h.shape
    q, k, v, g = jnp.split(h @ w_in, np.cumsum(ODD_SPLITS)[:-1].tolist(), axis=-1)
    q = rope(q.reshape(b, t, H_C, DK_C), pos)
    k = rope(k.reshape(b, t, H_C, DK_C), pos) * DK_C ** -0.5
    v = v.reshape(b, t, H_C, DV_C)
    o, s = retention_scan(q, k, v, s0)
    mu = jnp.mean(o, axis=-1, keepdims=True)
    var = jnp.mean(jnp.square(o - mu), axis=-1, keepdims=True)
    o = ((o - mu) * lax.rsqrt(var + EPS)).reshape(b, t, ODD_OUT).astype(h.dtype) * gn_gain
    return (jax.nn.silu(g) * o) @ w_out, s.astype(s0.dtype)


def hier_moe(h, w_rg, b_rg, w_re, b_re, w1, w3, w2):
    shp = h.shape
    hf = h.reshape(-1, D_MODEL)
    lg = (hf @ w_rg + b_rg).astype(F32)
    gsel = jnp.argmax(lg, axis=-1)
    gate_g = jnp.take_along_axis(jax.nn.softmax(lg, axis=-1), gsel[:, None], axis=1)
    le = (hf @ w_re + b_re).astype(F32).reshape(-1, N_GROUPS, E_PER_GROUP)
    le = jnp.take_along_axis(le, gsel[:, None, None], axis=1)[:, 0]
    top_p, top_i = lax.top_k(jax.nn.softmax(le, axis=-1), TOP_K_EXPERTS)
    wts = gate_g * top_p / jnp.sum(top_p, axis=-1, keepdims=True)
    eid = gsel[:, None] * E_PER_GROUP + top_i
    combine = jnp.sum(jax.nn.one_hot(eid, N_EXPERTS, dtype=F32) * wts[..., None], axis=1)
    a = jax.nn.silu(jnp.einsum('nd,edf->nef', hf, w1)) * jnp.einsum('nd,edf->nef', hf, w3)
    a = a * combine.astype(a.dtype)[..., None]
    return jnp.einsum('nef,efd->nd', a, w2).reshape(shp)


def setup_inputs(seed: int = 0) -> dict:
    key = jax.random.key(seed)
    ks = iter(jax.random.split(key, 40))
    n_even = (DEPTH + 1) // 2
    n_odd = DEPTH // 2
    n_pages = PAST_LEN // PAGE_SIZE
    n_used = DEC_BATCH * n_pages
    n_pool = n_used + max(1, n_used // 4)

    def nrm(shape, scale=1.0):
        return jax.random.normal(next(ks), shape, F32) * scale

    d = D_MODEL
    page_table = jax.random.permutation(next(ks), n_pool)[:n_used].reshape(DEC_BATCH, n_pages).astype(jnp.int32)
    return {
        'x_prompt': nrm((BATCH, SEQ, d)),
        'x_sample': nrm((DEC_BATCH, DEC_SEQ, d)),
        'cache_a_k': nrm((n_even, n_pool, PAGE_SIZE, KV_A, HD_A)),
        'cache_a_v': nrm((n_even, n_pool, PAGE_SIZE, KV_A, HD_A)),
        'cache_a_idx': nrm((n_even, n_pool, PAGE_SIZE, D_I)),
        'cache_b_latent': nrm((n_even, n_pool, PAGE_SIZE, KV_LORA)),
        'cache_b_rope': nrm((n_even, n_pool, PAGE_SIZE, ROPE_B)),
        'state_c': nrm((n_odd, DEC_BATCH, H_C, DK_C, DV_C)),
        'page_table': page_table,
        'c_prompt': nrm((BATCH, d)),
        'c_sample': nrm((DEC_BATCH, d)),
        'rel_bias': nrm((N_BUCKETS, H_A), 0.5),
        'w_ada': nrm((DEPTH, d, 6 * d), 0.5 * d ** -0.5),
        'b_ada': nrm((DEPTH, 6 * d), 0.1),
        'norm_mix': 1.0 + nrm((DEPTH, d), 0.1),
        'norm_ffn': 1.0 + nrm((DEPTH, d), 0.1),
        'norm_final': 1.0 + nrm((d,), 0.1),
        'w_in_even': nrm((n_even, d, EVEN_IN), d ** -0.5),
        'q_norm_b': 1.0 + nrm((n_even, Q_LORA), 0.1),
        'kv_norm_b': 1.0 + nrm((n_even, KV_LORA), 0.1),
        'w_uq_b': nrm((n_even, Q_LORA, H_B * (NOPE_B + ROPE_B)), Q_LORA ** -0.5),
        'w_uk_b': nrm((n_even, KV_LORA, H_B, NOPE_B), KV_LORA ** -0.5),
        'w_uv_b': nrm((n_even, KV_LORA, H_B, V_B), KV_LORA ** -0.5),
        'w_out_even': nrm((n_even, EVEN_OUT, d), EVEN_OUT ** -0.5),
        'w_in_odd': nrm((n_odd, d, ODD_IN), d ** -0.5),
        'gn_gain_c': 1.0 + nrm((n_odd, ODD_OUT), 0.1),
        'w_out_odd': nrm((n_odd, ODD_OUT, d), ODD_OUT ** -0.5),
        'w_rg': nrm((DEPTH, d, N_GROUPS), d ** -0.5),
        'b_rg': nrm((DEPTH, N_GROUPS), 0.01),
        'w_re': nrm((DEPTH, d, N_EXPERTS), d ** -0.5),
        'b_re': nrm((DEPTH, N_EXPERTS), 0.01),
        'w1': nrm((DEPTH, N_EXPERTS, d, D_EXPERT), d ** -0.5),
        'w3': nrm((DEPTH, N_EXPERTS, d, D_EXPERT), d ** -0.5),
        'w2': nrm((DEPTH, N_EXPERTS, D_EXPERT, d), D_EXPERT ** -0.5),
    }


def reference(x_prompt, x_sample, cache_a_k, cache_a_v, cache_a_idx, cache_b_latent, cache_b_rope, state_c,
              page_table, c_prompt, c_sample, rel_bias, w_ada, b_ada, norm_mix, norm_ffn, norm_final,
              w_in_even, q_norm_b, kv_norm_b, w_uq_b, w_uk_b, w_uv_b, w_out_even, w_in_odd, gn_gain_c,
              w_out_odd, w_rg, b_rg, w_re, b_re, w1, w3, w2):
    topk_p = min(TOPK_MAX, SEQ // 4)
    topk_s = min(TOPK_MAX, (PAST_LEN + DEC_SEQ) // 4)
    pos_p = jnp.arange(x_prompt.shape[1], dtype=jnp.int32)
    pos_s = PAST_LEN + jnp.arange(x_sample.shape[1], dtype=jnp.int32)
    xp, xs = x_prompt, x_sample
    pk, pv, pidx, plat, prope, pst = [], [], [], [], [], []
    sk, sv, sidx, slat, srope, sst = [], [], [], [], [], []
    for l in range(DEPTH):
        mp = ada_params(c_prompt, w_ada[l], b_ada[l])
        ms = ada_params(c_sample, w_ada[l], b_ada[l])
        hp = modulate(rmsnorm(xp, norm_mix[l]), mp[:, 0], mp[:, 1])
        hs = modulate(rmsnorm(xs, norm_mix[l]), ms[:, 0], ms[:, 1])
        if l % 2 == 0:
            e = l // 2
            proj_w = (w_in_even[e], q_norm_b[e], kv_norm_b[e], w_uq_b[e], w_uk_b[e])
            qa, ka, va, iq, ik, iw, ql, qpe, ckv, kpe = even_project(hp, pos_p, *proj_w)
            yp = even_mix((qa, iq, iw, ql, qpe), pos_p, ik, ckv, kpe,
                          functools.partial(prompt_fetch, ka, va), topk_p, rel_bias, w_uv_b[e], w_out_even[e])
            pk.append(ka); pv.append(va); pidx.append(ik); plat.append(ckv); prope.append(kpe)
            qa_s, ka_s, va_s, iq_s, ik_s, iw_s, ql_s, qpe_s, ckv_s, kpe_s = even_project(hs, pos_s, *proj_w)
            ik_all = jnp.concatenate([paged_rows(cache_a_idx, e, page_table), ik_s], axis=1)
            ckv_all = jnp.concatenate([paged_rows(cache_b_latent, e, page_table), ckv_s], axis=1)
            kpe_all = jnp.concatenate([paged_rows(cache_b_rope, e, page_table), kpe_s], axis=1)
            fetch_s = functools.partial(paged_fetch, cache_a_k, cache_a_v, e, page_table, ka_s, va_s)
            ys = even_mix((qa_s, iq_s, iw_s, ql_s, qpe_s), pos_s, ik_all, ckv_all, kpe_all,
                          fetch_s, topk_s, rel_bias, w_uv_b[e], w_out_even[e])
            sk.append(ka_s); sv.append(va_s); sidx.append(ik_s); slat.append(ckv_s); srope.append(kpe_s)
        else:
            o = l // 2
            s0 = jnp.zeros((xp.shape[0], H_C, DK_C, DV_C), xp.dtype)
            yp, s_p = retention_mixer(hp, pos_p, s0, w_in_odd[o], gn_gain_c[o], w_out_odd[o])
            ys, s_s = retention_mixer(hs, pos_s, state_c[o], w_in_odd[o], gn_gain_c[o], w_out_odd[o])
            pst.append(s_p); sst.append(s_s)
        xp = xp + mp[:, 2] * yp
        xs = xs + ms[:, 2] * ys
        moe_w = (w_rg[l], b_rg[l], w_re[l], b_re[l], w1[l], w3[l], w2[l])
        xp = xp + mp[:, 5] * hier_moe(modulate(rmsnorm(xp, norm_ffn[l]), mp[:, 3], mp[:, 4]), *moe_w)
        xs = xs + ms[:, 5] * hier_moe(modulate(rmsnorm(xs, norm_ffn[l]), ms[:, 3], ms[:, 4]), *moe_w)
    y_prompt = rmsnorm(xp, norm_final)
    y_sample = rmsnorm(xs, norm_final)
    return (y_prompt, y_sample,
            jnp.stack(pk), jnp.stack(pv), jnp.stack(pidx), jnp.stack(plat), jnp.stack(prope), jnp.stack(pst),
            jnp.stack(sk), jnp.stack(sv), jnp.stack(sidx), jnp.stack(slat), jnp.stack(srope), jnp.stack(sst))
```

```python
import functools
import math

import numpy as np
import jax
import jax.numpy as jnp
from jax import lax
from jax.experimental import pallas as pl
from jax.experimental.pallas import tpu as pltpu

F32 = jnp.float32
BF16 = jnp.bfloat16
I32 = jnp.int32

D_MODEL = 2048
DEPTH = 4
PAST_LEN = 8192
PAGE_SIZE = 128
H_A, HD_A, KV_A = 8, 128, 2
H_I, D_I = 16, 64
TOPK_MAX = 256
N_BUCKETS, MAX_DISTANCE = 32, 128
H_B, Q_LORA, KV_LORA, NOPE_B, ROPE_B, V_B = 8, 512, 256, 128, 64, 128
ROPE_BASE = 10000.0
H_C = 8
DK_C = D_MODEL // H_C
DV_C = 2 * D_MODEL // H_C
RET_CHUNK = 128
N_GROUPS, E_PER_GROUP = 4, 8
N_EXPERTS = N_GROUPS * E_PER_GROUP
D_EXPERT = 512
EPS = 1e-6

LANES = 128
SUBLANES = 8
VMEM_LIMIT = 52 * 1024 * 1024

EV_QA, EV_IQ, EV_CQ, EV_KA, EV_VA, EV_CKV, EV_IK, EV_KPE, EV_IW = (
    0, 1024, 2048, 2560, 2816, 3072, 3328, 3392, 3456)
EV_COLS = 3584
QBLK = 128
NEG_INF = float("-inf")


def _cparams(sem, vmem=VMEM_LIMIT):
    return pltpu.CompilerParams(dimension_semantics=sem, vmem_limit_bytes=vmem)


def _bdot(a, b):
    return jnp.dot(a.astype(BF16), b.astype(BF16), preferred_element_type=F32)


def _bdot_nt(a, b):
    return lax.dot_general(a.astype(BF16), b.astype(BF16), (((1,), (1,)), ((), ())),
                           preferred_element_type=F32)


def _bdot_tn(a, b):
    return lax.dot_general(a.astype(BF16), b.astype(BF16), (((0,), (0,)), ((), ())),
                           preferred_element_type=F32)


def _rms(x):
    return x * lax.rsqrt(jnp.mean(x * x, axis=-1, keepdims=True) + EPS)


def _ada_kernel(c_ref, w_ref, b_ref, o_ref):
    c = c_ref[...]
    a = c * jax.nn.sigmoid(c)
    o_ref[...] = _bdot(a, w_ref[...]) + b_ref[...]


def ada_all(c, w_ada, b_ada):
    r, d = c.shape
    nl, _, n = w_ada.shape
    tn = 1024
    return pl.pallas_call(
        _ada_kernel,
        grid=(nl, n // tn),
        in_specs=[pl.BlockSpec((r, d), lambda l, j: (0, 0)),
                  pl.BlockSpec((None, d, tn), lambda l, j: (l, 0, j)),
                  pl.BlockSpec((None, 1, tn), lambda l, j: (l, 0, j))],
        out_specs=pl.BlockSpec((None, r, tn), lambda l, j: (l, 0, j)),
        out_shape=jax.ShapeDtypeStruct((nl, r, n), F32),
        compiler_params=_cparams(("arbitrary", "arbitrary")),
        name="ada_all",
    )(c, w_ada, b_ada.reshape(nl, 1, n))


def _nm_matmul_kernel(x_ref, g_ref, sh_ref, sc_ref, w_ref, o_ref, h_scr):
    @pl.when(pl.program_id(1) == 0)
    def _():
        h = (_rms(x_ref[...]) * g_ref[...]) * (1.0 + sc_ref[...]) + sh_ref[...]
        h_scr[...] = h.astype(BF16)
    o_ref[...] = jnp.dot(h_scr[...], w_ref[...].astype(BF16), preferred_element_type=F32)


def _per_block(mod, tm):
    if mod.shape[1] == 1:
        return mod
    return mod.reshape(-1, tm, mod.shape[-1])


def _mod_spec(mod, tm, rows_per_group):
    steps = max(rows_per_group // tm, 1)
    return pl.BlockSpec((None,) + mod.shape[1:], lambda i, j: (i // steps, 0, 0))


def nm_matmul(x, gain, shift, scale, w, rows_per_group, tm=512, tn=512):
    m, d = x.shape
    n = w.shape[1]
    tm = min(tm, m)
    shift, scale = _per_block(shift, tm), _per_block(scale, tm)
    return pl.pallas_call(
        _nm_matmul_kernel,
        grid=(m // tm, n // tn),
        in_specs=[pl.BlockSpec((tm, d), lambda i, j: (i, 0)),
                  pl.BlockSpec((1, d), lambda i, j: (0, 0)),
                  _mod_spec(shift, tm, rows_per_group),
                  _mod_spec(scale, tm, rows_per_group),
                  pl.BlockSpec((d, tn), lambda i, j: (0, j))],
        out_specs=pl.BlockSpec((tm, tn), lambda i, j: (i, j)),
        out_shape=jax.ShapeDtypeStruct((m, n), F32),
        scratch_shapes=[pltpu.VMEM((tm, d), BF16)],
        compiler_params=_cparams(("arbitrary", "arbitrary")),
        name="nm_matmul",
    )(x, gain.reshape(1, d), shift, scale, w)


def _mm_res_kernel(a_ref, w_ref, x_ref, gate_ref, o_ref):
    o_ref[...] = x_ref[...] + gate_ref[...] * _bdot(a_ref[...], w_ref[...])


def mm_residual(a, w, x, gate, rows_per_group, tm=512, tn=512):
    m, k = a.shape
    n = w.shape[1]
    tm = min(tm, m)
    steps = max(rows_per_group // tm, 1)
    gate = _per_block(gate, tm)
    return pl.pallas_call(
        _mm_res_kernel,
        grid=(m // tm, n // tn),
        in_specs=[pl.BlockSpec((tm, k), lambda i, j: (i, 0)),
                  pl.BlockSpec((k, tn), lambda i, j: (0, j)),
                  pl.BlockSpec((tm, tn), lambda i, j: (i, j)),
                  pl.BlockSpec((None, gate.shape[1], tn), lambda i, j: (i // steps, 0, j))],
        out_specs=pl.BlockSpec((tm, tn), lambda i, j: (i, j)),
        out_shape=jax.ShapeDtypeStruct((m, n), F32),
        compiler_params=_cparams(("arbitrary", "arbitrary")),
        name="mm_residual",
    )(a, w, x, gate)


def _swap_halves(x, half):
    n = x.shape[-1]
    lane = lax.broadcasted_iota(I32, x.shape, x.ndim - 1)
    return jnp.where(lane % (2 * half) < half,
                     pltpu.roll(x, n - half, x.ndim - 1), pltpu.roll(x, half, x.ndim - 1))


def _even_post_kernel(cq_ref, ckv_ref, slab_ref, qn_ref, kvn_ref, wuq_ref, wuk_ref,
                      cq_c_ref, cq_s_ref, cs_c_ref, cs_s_ref,
                      qlat_ref, qpe_ref, ckvn_ref, ikkpe_ref):
    cqn = _rms(cq_ref[...]) * qn_ref[...]
    qb = _bdot(cqn, wuq_ref[...])
    nope = H_B * NOPE_B
    for h in range(H_B):
        qlat_ref[:, h * KV_LORA:(h + 1) * KV_LORA] = _bdot(
            qb[:, h * NOPE_B:(h + 1) * NOPE_B], wuk_ref[h])
    qr = qb[:, nope:]
    qpe_ref[...] = qr * cq_c_ref[...] + _swap_halves(qr, ROPE_B // 2) * cq_s_ref[...]
    ckvn_ref[...] = _rms(ckv_ref[...]) * kvn_ref[...]
    slab = slab_ref[:, :LANES]
    ikkpe_ref[...] = slab * cs_c_ref[...] + _swap_halves(slab, ROPE_B // 2) * cs_s_ref[...]


def even_post(proj, q_norm, kv_norm, w_uq_r, w_uk_t, tabs, tm=256):
    m = proj.shape[0]
    tm = min(tm, m)
    cq_c, cq_s, cs_c, cs_s = tabs
    nt = cq_c.shape[0] // tm
    tab = lambda w: pl.BlockSpec((tm, w), lambda i: (i % nt, 0))
    full = lambda a: pl.BlockSpec(a.shape, lambda i: (0,) * a.ndim)
    return pl.pallas_call(
        _even_post_kernel,
        grid=(m // tm,),
        in_specs=[pl.BlockSpec((tm, Q_LORA), lambda i: (i, EV_CQ // Q_LORA)),
                  pl.BlockSpec((tm, KV_LORA), lambda i: (i, EV_CKV // KV_LORA)),
                  pl.BlockSpec((tm, 256), lambda i: (i, EV_IK // 256)),
                  full(q_norm), full(kv_norm), full(w_uq_r), full(w_uk_t),
                  tab(512), tab(512), tab(LANES), tab(LANES)],
        out_specs=[pl.BlockSpec((tm, H_B * KV_LORA), lambda i: (i, 0)),
                   pl.BlockSpec((tm, H_B * ROPE_B), lambda i: (i, 0)),
                   pl.BlockSpec((tm, KV_LORA), lambda i: (i, 0)),
                   pl.BlockSpec((tm, LANES), lambda i: (i, 0))],
        out_shape=[jax.ShapeDtypeStruct((m, H_B * KV_LORA), F32),
                   jax.ShapeDtypeStruct((m, H_B * ROPE_B), F32),
                   jax.ShapeDtypeStruct((m, KV_LORA), F32),
                   jax.ShapeDtypeStruct((m, LANES), F32)],
        compiler_params=_cparams(("arbitrary",)),
        name="even_post",
    )(proj, proj, proj, q_norm, kv_norm, w_uq_r, w_uk_t, cq_c, cq_s, cs_c, cs_s)


def _topk_mask(score, valid, kpos, topk, idx_bits):
    sm = jnp.where(valid, jnp.where(score == 0.0, 0.0, score), NEG_INF)
    bits = pltpu.bitcast(sm, I32)
    key = jnp.where(bits < 0, bits ^ jnp.int32(0x7FFFFFFF), bits)

    def count(m):
        return jnp.sum(m.astype(I32), axis=1, keepdims=True)

    int_min = jnp.int32(-2 ** 31)
    t0 = jnp.where(count(key >= 0) >= topk, jnp.int32(0), int_min)

    def bit_step(i, t):
        cand = t | (jnp.int32(1) << (30 - i))
        return jnp.where(count(key >= cand) >= topk, cand, t)

    thr = lax.fori_loop(0, 31, bit_step, jnp.broadcast_to(t0, (score.shape[0], 1)))
    above = key > thr
    ties = key == thr
    need = topk - count(above)

    def idx_step(i, j):
        cand = j | (jnp.int32(1) << (idx_bits - 1 - i))
        return jnp.where(count(ties & (kpos < cand)) < need, cand, j)

    last = lax.fori_loop(0, idx_bits, idx_step, jnp.zeros((score.shape[0], 1), I32))
    return above | (ties & (kpos <= last))


def _prompt_mix_kernel(qa_ref, iq_ref, qslab_ref, qlat_ref, qpe_ref,
                       ka_ref, va_ref, ckv_ref, ikkpe_ref, tz_ref, wuv_ref,
                       o_ref, s_scr, *, topk, idx_bits):
    qi = pl.program_id(1)
    tq = qa_ref.shape[0]
    nk = ka_ref.shape[0]
    q0 = pl.multiple_of(qi * tq, tq)
    qpos = q0 + lax.broadcasted_iota(I32, (tq, nk), 0)
    kpos = lax.broadcasted_iota(I32, (tq, nk), 1)
    causal = kpos <= qpos

    ikb = ikkpe_ref[:, :D_I].astype(BF16)
    kpeb = ikkpe_ref[:, D_I:].astype(BF16)
    wq = qslab_ref[:, LANES:LANES + H_I] * (H_I ** -0.5 * D_I ** -0.5)
    score = jnp.zeros((tq, nk), F32)
    for h in range(H_I):
        s = _bdot_nt(iq_ref[:, h * D_I:(h + 1) * D_I], ikb)
        score = score + wq[:, h:h + 1] * jnp.maximum(s, 0.0)
    sel = _topk_mask(score, causal, kpos, topk, idx_bits) & causal

    grp = H_A // KV_A
    for kv in range(KV_A):
        kk = ka_ref[:, kv * HD_A:(kv + 1) * HD_A].astype(BF16)
        vv = va_ref[:, kv * HD_A:(kv + 1) * HD_A].astype(BF16)
        for g in range(grp):
            h = kv * grp + g
            s_scr[...] = _bdot_nt(qa_ref[:, h * HD_A:(h + 1) * HD_A], kk) * (HD_A ** -0.5)
            s_scr[:, pl.ds(q0, tq)] += tz_ref[h, :, tq:]

            @pl.when(qi > 0)
            def _():
                s_scr[:, pl.ds(q0 - tq, tq)] += tz_ref[h, :, :tq]

            s = jnp.where(sel, s_scr[...], NEG_INF)
            p = jnp.exp(s - jnp.max(s, axis=1, keepdims=True))
            p = p / jnp.sum(p, axis=1, keepdims=True)
            o_ref[:, h * HD_A:(h + 1) * HD_A] = _bdot(p, vv).astype(o_ref.dtype)

    ckvb = ckv_ref[...].astype(BF16)
    off = H_A * HD_A
    for h in range(H_B):
        s = (_bdot_nt(qlat_ref[:, h * KV_LORA:(h + 1) * KV_LORA], ckvb)
             + _bdot_nt(qpe_ref[:, h * ROPE_B:(h + 1) * ROPE_B], kpeb)) * ((NOPE_B + ROPE_B) ** -0.5)
        s = jnp.where(causal, s, NEG_INF)
        p = jnp.exp(s - jnp.max(s, axis=1, keepdims=True))
        p = p / jnp.sum(p, axis=1, keepdims=True)
        o_lat = _bdot(p, ckvb)
        o_ref[:, off + h * V_B:off + (h + 1) * V_B] = _bdot(
            o_lat, wuv_ref[:, h * V_B:(h + 1) * V_B]).astype(o_ref.dtype)


def prompt_mix(proj, qlat, qpe, ckvn, ikkpe, tz, w_uv2, nb, seq, topk):
    tq = QBLK
    nq = seq // tq
    row = lambda w, c: pl.BlockSpec((tq, w), lambda b, q: (b * nq + q, c))
    keys = lambda w, c: pl.BlockSpec((seq, w), lambda b, q: (b, c))
    full = lambda a: pl.BlockSpec(a.shape, lambda b, q: (0,) * a.ndim)
    kern = functools.partial(_prompt_mix_kernel, topk=topk,
                             idx_bits=max(int(math.ceil(math.log2(seq))), 1))
    return pl.pallas_call(
        kern,
        grid=(nb, nq),
        in_specs=[row(1024, EV_QA // 1024), row(1024, EV_IQ // 1024), row(256, EV_IK // 256),
                  row(H_B * KV_LORA, 0), row(H_B * ROPE_B, 0),
                  keys(256, EV_KA // 256), keys(256, EV_VA // 256),
                  keys(KV_LORA, 0), keys(LANES, 0), full(tz), full(w_uv2)],
        out_specs=pl.BlockSpec((tq, 2048), lambda b, q: (b * nq + q, 0)),
        out_shape=jax.ShapeDtypeStruct((nb * seq, 2048), BF16),
        scratch_shapes=[pltpu.VMEM((tq, seq), F32)],
        compiler_params=_cparams(("arbitrary", "arbitrary")),
        name="prompt_mix",
    )(proj, proj, proj, qlat, qpe, proj, proj, ckvn, ikkpe, tz, w_uv2)


def _page_copy(pt_ref, cache_ref, layer, b, j, buf, slot, sem, col0, width):
    src = cache_ref.at[layer, pt_ref[b, j]]
    if width is not None:
        src = src.at[:, pl.ds(col0, width)]
    return pltpu.make_async_copy(src, buf.at[slot, pl.ds(j * PAGE_SIZE, PAGE_SIZE)], sem.at[slot])


def _fetch_pages(step, n_steps, step_coords, copies):
    slot = step % 2

    @pl.when(step == 0)
    def _():
        for c in copies(step_coords(step), slot):
            c.start()

    @pl.when(step + 1 < n_steps)
    def _():
        for c in copies(step_coords(step + 1), 1 - slot):
            c.start()

    for c in copies(step_coords(step), slot):
        c.wait()
    return slot


def _sample_select_kernel(pt_ref, q_ref, w_ref, new_ref, cache_ref, mask_ref, ibuf, s_scr, sem,
                          *, layer, n_pages, n_new, topk, idx_bits, chunk):
    def copies(b, slot):
        return [_page_copy(pt_ref, cache_ref, layer, b, j, ibuf, slot, sem, 0, None)
                for j in range(n_pages)]

    slot = _fetch_pages(pl.program_id(0), pl.num_programs(0), lambda s: s, copies)
    past = n_pages * PAGE_SIZE
    n = past + PAGE_SIZE
    qb = q_ref[...].astype(BF16)
    wq = w_ref[...]

    def scores(keys):
        s = jnp.maximum(_bdot_nt(qb, keys), 0.0) * wq
        acc = s[0:SUBLANES]
        for h in range(1, H_I):
            acc = acc + s[h * SUBLANES:(h + 1) * SUBLANES]
        return acc

    for c0 in range(0, past, chunk):
        s_scr[:, c0:c0 + chunk] = scores(ibuf[slot, c0:c0 + chunk, :])
    s_scr[:, past:] = scores(new_ref[...])
    tok = lax.broadcasted_iota(I32, (SUBLANES, n), 0)
    kpos = lax.broadcasted_iota(I32, (SUBLANES, n), 1)
    valid = (kpos <= past + tok) & (kpos < past + n_new)
    sel = _topk_mask(s_scr[...], valid, kpos, topk, idx_bits) & valid
    mask_ref[...] = sel.astype(F32)


def sample_select(page_table, q_idx, w_idx, cache_idx, layer, ik_new_pad, n_new, topk):
    nb, n_pages = page_table.shape
    past = n_pages * PAGE_SIZE
    n = past + PAGE_SIZE
    kern = functools.partial(_sample_select_kernel, layer=layer, n_pages=n_pages, n_new=n_new,
                             topk=topk, idx_bits=int(math.ceil(math.log2(n))),
                             chunk=min(past, 1024))
    grid_spec = pltpu.PrefetchScalarGridSpec(
        num_scalar_prefetch=1,
        grid=(nb,),
        in_specs=[pl.BlockSpec((None, H_I * SUBLANES, D_I), lambda b, pt: (b, 0, 0)),
                  pl.BlockSpec((None, H_I * SUBLANES, 1), lambda b, pt: (b, 0, 0)),
                  pl.BlockSpec((None, PAGE_SIZE, D_I), lambda b, pt: (b, 0, 0)),
                  pl.BlockSpec(memory_space=pl.ANY)],
        out_specs=pl.BlockSpec((None, SUBLANES, n), lambda b, pt: (b, 0, 0)),
        scratch_shapes=[pltpu.VMEM((2, past, D_I), F32), pltpu.VMEM((SUBLANES, n), F32),
                        pltpu.SemaphoreType.DMA((2,))],
    )
    return pl.pallas_call(
        kern, grid_spec=grid_spec,
        out_shape=jax.ShapeDtypeStruct((nb, SUBLANES, n), F32),
        compiler_params=_cparams(("arbitrary",)),
        name="sample_select",
    )(page_table, q_idx, w_idx, ik_new_pad, cache_idx)


def _sample_dsa_kernel(pt_ref, q_ref, mask_ref, tz_ref, knew_ref, vnew_ref, ck_ref, cv_ref, o_ref,
                       kbuf, vbuf, k_scr, v_scr, sem_k, sem_v, *, layer, n_pages):
    def coords(s):
        return s // KV_A, s % KV_A

    def copies(bc, slot):
        b, c = bc
        col0 = pl.multiple_of(c * HD_A, HD_A)
        out = []
        for j in range(n_pages):
            out.append(_page_copy(pt_ref, ck_ref, layer, b, j, kbuf, slot, sem_k, col0, HD_A))
            out.append(_page_copy(pt_ref, cv_ref, layer, b, j, vbuf, slot, sem_v, col0, HD_A))
        return out

    step = pl.program_id(0) * KV_A + pl.program_id(1)
    slot = _fetch_pages(step, pl.num_programs(0) * KV_A, coords, copies)
    past = n_pages * PAGE_SIZE
    n = past + PAGE_SIZE
    k_scr[:past, :] = kbuf[slot].astype(BF16)
    v_scr[:past, :] = vbuf[slot].astype(BF16)
    k_scr[past:, :] = knew_ref[...].astype(BF16)
    v_scr[past:, :] = vnew_ref[...].astype(BF16)
    grp = H_A // KV_A
    s = _bdot_nt(q_ref[...], k_scr[...]) * (HD_A ** -0.5)
    near = 2 * PAGE_SIZE
    bias = jnp.concatenate([jnp.zeros((grp * SUBLANES, n - near), F32), tz_ref[...]], axis=1)
    m = mask_ref[...]
    sel = jnp.concatenate([m] * grp, axis=0) > 0.5
    s = jnp.where(sel, s + bias, NEG_INF)
    p = jnp.exp(s - jnp.max(s, axis=1, keepdims=True))
    p = p / jnp.sum(p, axis=1, keepdims=True)
    o_ref[...] = _bdot(p, v_scr[...])


def sample_dsa(page_table, q_a, mask, tz_s, cache_k, cache_v, layer, k_new_pad, v_new_pad):
    nb, n_pages = page_table.shape
    past = n_pages * PAGE_SIZE
    n = past + PAGE_SIZE
    grp = H_A // KV_A
    rows = grp * SUBLANES
    kern = functools.partial(_sample_dsa_kernel, layer=layer, n_pages=n_pages)
    new_spec = pl.BlockSpec((None, PAGE_SIZE, HD_A), lambda b, c, pt: (b, 0, c))
    grid_spec = pltpu.PrefetchScalarGridSpec(
        num_scalar_prefetch=1,
        grid=(nb, KV_A),
        in_specs=[pl.BlockSpec((None, None, rows, HD_A), lambda b, c, pt: (b, c, 0, 0)),
                  pl.BlockSpec((None, SUBLANES, n), lambda b, c, pt: (b, 0, 0)),
                  pl.BlockSpec((None, rows, 2 * PAGE_SIZE), lambda b, c, pt: (c, 0, 0)),
                  new_spec, new_spec,
                  pl.BlockSpec(memory_space=pl.ANY), pl.BlockSpec(memory_space=pl.ANY)],
        out_specs=pl.BlockSpec((None, None, rows, HD_A), lambda b, c, pt: (b, c, 0, 0)),
        scratch_shapes=[pltpu.VMEM((2, past, HD_A), F32), pltpu.VMEM((2, past, HD_A), F32),
                        pltpu.VMEM((n, HD_A), BF16), pltpu.VMEM((n, HD_A), BF16),
                        pltpu.SemaphoreType.DMA((2,)), pltpu.SemaphoreType.DMA((2,))],
    )
    return pl.pallas_call(
        kern, grid_spec=grid_spec,
        out_shape=jax.ShapeDtypeStruct((nb, KV_A, rows, HD_A), F32),
        compiler_params=_cparams(("arbitrary", "arbitrary")),
        name="sample_dsa",
    )(page_table, q_a, mask, tz_s, k_new_pad, v_new_pad, cache_k, cache_v)


def _sample_mla_kernel(pt_ref, qlat_ref, qpe_ref, cnew_ref, rnew_ref, cc_ref, cr_ref, o_ref,
                       cbuf, rbuf, c_scr, r_scr, sem_c, sem_r, *, layer, n_pages, n_new):
    def copies(b, slot):
        out = []
        for j in range(n_pages):
            out.append(_page_copy(pt_ref, cc_ref, layer, b, j, cbuf, slot, sem_c, 0, None))
            out.append(_page_copy(pt_ref, cr_ref, layer, b, j, rbuf, slot, sem_r, 0, None))
        return out

    slot = _fetch_pages(pl.program_id(0), pl.num_programs(0), lambda s: s, copies)
    past = n_pages * PAGE_SIZE
    n = past + PAGE_SIZE
    c_scr[:past, :] = cbuf[slot].astype(BF16)
    r_scr[:past, :] = rbuf[slot].astype(BF16)
    c_scr[past:, :] = cnew_ref[...].astype(BF16)
    r_scr[past:, :] = rnew_ref[...].astype(BF16)
    rows = H_B * SUBLANES
    s = (_bdot_nt(qlat_ref[...], c_scr[...]) + _bdot_nt(qpe_ref[...], r_scr[...])) * (
        (NOPE_B + ROPE_B) ** -0.5)
    tok = lax.broadcasted_iota(I32, (rows, n), 0) % SUBLANES
    kpos = lax.broadcasted_iota(I32, (rows, n), 1)
    valid = (kpos <= past + tok) & (kpos < past + n_new)
    s = jnp.where(valid, s, NEG_INF)
    p = jnp.exp(s - jnp.max(s, axis=1, keepdims=True))
    p = p / jnp.sum(p, axis=1, keepdims=True)
    o_ref[...] = _bdot(p, c_scr[...])


def sample_mla(page_table, q_lat, q_pe, cache_lat, cache_rope, layer, c_new_pad, r_new_pad, n_new):
    nb, n_pages = page_table.shape
    past = n_pages * PAGE_SIZE
    n = past + PAGE_SIZE
    rows = H_B * SUBLANES
    kern = functools.partial(_sample_mla_kernel, layer=layer, n_pages=n_pages, n_new=n_new)
    grid_spec = pltpu.PrefetchScalarGridSpec(
        num_scalar_prefetch=1,
        grid=(nb,),
        in_specs=[pl.BlockSpec((None, rows, KV_LORA), lambda b, pt: (b, 0, 0)),
                  pl.BlockSpec((None, rows, ROPE_B), lambda b, pt: (b, 0, 0)),
                  pl.BlockSpec((None, PAGE_SIZE, KV_LORA), lambda b, pt: (b, 0, 0)),
                  pl.BlockSpec((None, PAGE_SIZE, ROPE_B), lambda b, pt: (b, 0, 0)),
                  pl.BlockSpec(memory_space=pl.ANY), pl.BlockSpec(memory_space=pl.ANY)],
        out_specs=pl.BlockSpec((None, rows, KV_LORA), lambda b, pt: (b, 0, 0)),
        scratch_shapes=[pltpu.VMEM((2, past, KV_LORA), F32), pltpu.VMEM((2, past, ROPE_B), F32),
                        pltpu.VMEM((n, KV_LORA), BF16), pltpu.VMEM((n, ROPE_B), BF16),
                        pltpu.SemaphoreType.DMA((2,)), pltpu.SemaphoreType.DMA((2,))],
    )
    return pl.pallas_call(
        kern, grid_spec=grid_spec,
        out_shape=jax.ShapeDtypeStruct((nb, rows, KV_LORA), F32),
        compiler_params=_cparams(("arbitrary",)),
        name="sample_mla",
    )(page_table, q_lat, q_pe, c_new_pad, r_new_pad, cache_lat, cache_rope)


def _head_proj_kernel(o_ref, w_ref, y_ref):
    y_ref[...] = _bdot(o_ref[...], w_ref[...])


def head_proj(o_lat, w_uv_h):
    nh, m, c = o_lat.shape
    v = w_uv_h.shape[2]
    return pl.pallas_call(
        _head_proj_kernel,
        grid=(nh,),
        in_specs=[pl.BlockSpec((None, m, c), lambda h: (h, 0, 0)),
                  pl.BlockSpec((None, c, v), lambda h: (h, 0, 0))],
        out_specs=pl.BlockSpec((None, m, v), lambda h: (h, 0, 0)),
        out_shape=jax.ShapeDtypeStruct((nh, m, v), F32),
        compiler_params=_cparams(("arbitrary",)),
        name="head_proj",
    )(o_lat, w_uv_h)


def _rope_half(x, cos, sin):
    half = x.shape[-1] // 2
    x1, x2 = x[:, :half], x[:, half:]
    return jnp.concatenate([x1 * cos - x2 * sin, x1 * sin + x2 * cos], axis=-1)


def _group_norm_gate(o, g, gain):
    mu = jnp.mean(o, axis=-1, keepdims=True)
    d = o - mu
    var = jnp.mean(d * d, axis=-1, keepdims=True)
    on = (d * lax.rsqrt(var + EPS)) * gain
    return (g * jax.nn.sigmoid(g)) * on


def _ret_prompt_kernel(q_ref, k_ref, v_ref, g_ref, cos_ref, sin_ref, dmask_ref, cross_ref,
                       kdec_ref, cdec_ref, gain_ref, og_ref, st_ref, s_scr):
    c = pl.program_id(2)

    @pl.when(c == 0)
    def _():
        s_scr[...] = jnp.zeros_like(s_scr)

    cos, sin = cos_ref[...], sin_ref[...]
    q = _rope_half(q_ref[...], cos, sin)
    k = _rope_half(k_ref[...], cos, sin) * (DK_C ** -0.5)
    v = v_ref[...]
    s_old = s_scr[...]
    a = _bdot_nt(q, k) * dmask_ref[...]
    o = _bdot(a, v) + _bdot(q, s_old) * cross_ref[...]
    s_new = s_old * cdec_ref[...] + _bdot_tn(k * kdec_ref[...], v)
    s_scr[...] = s_new
    og_ref[...] = _group_norm_gate(o, g_ref[...], gain_ref[...]).astype(og_ref.dtype)

    @pl.when(c == pl.num_programs(2) - 1)
    def _():
        st_ref[...] = s_new


def ret_prompt(proj, tabs, gain, nb, seq):
    cos, sin, dmask, cross, kdec, cdec = tabs
    ch = dmask.shape[1]
    nc = seq // ch
    hq = H_C * DK_C
    kern = _ret_prompt_kernel
    per_head = lambda a: pl.BlockSpec((None,) + a.shape[1:], lambda b, h, c: (h,) + (0,) * (a.ndim - 1))
    return pl.pallas_call(
        kern,
        grid=(nb, H_C, nc),
        in_specs=[pl.BlockSpec((ch, DK_C), lambda b, h, c: (b * nc + c, h)),
                  pl.BlockSpec((ch, DK_C), lambda b, h, c: (b * nc + c, hq // DK_C + h)),
                  pl.BlockSpec((ch, DV_C), lambda b, h, c: (b * nc + c, 2 * hq // DV_C + h)),
                  pl.BlockSpec((ch, DV_C), lambda b, h, c: (b * nc + c, 2 * hq // DV_C + H_C + h)),
                  pl.BlockSpec((ch, DK_C // 2), lambda b, h, c: (c, 0)),
                  pl.BlockSpec((ch, DK_C // 2), lambda b, h, c: (c, 0)),
                  per_head(dmask), per_head(cross), per_head(kdec), per_head(cdec),
                  pl.BlockSpec((1, DV_C), lambda b, h, c: (0, h))],
        out_specs=[pl.BlockSpec((ch, DV_C), lambda b, h, c: (b * nc + c, h)),
                   pl.BlockSpec((None, None, DK_C, DV_C), lambda b, h, c: (b, h, 0, 0))],
        out_shape=[jax.ShapeDtypeStruct((nb * seq, H_C * DV_C), BF16),
                   jax.ShapeDtypeStruct((nb, H_C, DK_C, DV_C), F32)],
        scratch_shapes=[pltpu.VMEM((DK_C, DV_C), F32)],
        compiler_params=_cparams(("arbitrary", "arbitrary", "arbitrary")),
        name="ret_prompt",
    )(proj, proj, proj, proj, cos, sin, dmask, cross, kdec, cdec, gain)


def _ret_sample_kernel(p_ref, s0_ref, cos_ref, sin_ref, dmask_ref, cross_ref, kdec_ref, cdec_ref,
                       gain_ref, *refs):
    og_ref, st_ref = refs[-2:]
    cos, sin = cos_ref[...], sin_ref[...]
    hq = H_C * DK_C
    for h in range(H_C):
        q = _rope_half(p_ref[:, h * DK_C:(h + 1) * DK_C], cos, sin)
        k = _rope_half(p_ref[:, hq + h * DK_C:hq + (h + 1) * DK_C], cos, sin) * (DK_C ** -0.5)
        v = p_ref[:, 2 * hq + h * DV_C:2 * hq + (h + 1) * DV_C]
        g = p_ref[:, 2 * hq + (H_C + h) * DV_C:2 * hq + (H_C + h + 1) * DV_C]
        s_old = s0_ref[h]
        a = _bdot_nt(q, k) * dmask_ref[h]
        o = _bdot(a, v) + _bdot(q, s_old) * cross_ref[h]
        st_ref[h] = s_old * cdec_ref[h] + _bdot_tn(k * kdec_ref[h], v)
        og_ref[:, h * DV_C:(h + 1) * DV_C] = _group_norm_gate(
            o, g, gain_ref[:, h * DV_C:(h + 1) * DV_C]).astype(og_ref.dtype)


def ret_sample(proj_pad, state_c, layer, tabs, gain, st_all):
    cos, sin, dmask, cross, kdec, cdec = tabs
    nb = proj_pad.shape[0]
    full = lambda a: pl.BlockSpec(a.shape, lambda b: (0,) * a.ndim)
    ins = [proj_pad, state_c, cos, sin, dmask, cross, kdec, cdec, gain]
    in_specs = [pl.BlockSpec((None, SUBLANES, proj_pad.shape[2]), lambda b: (b, 0, 0)),
                pl.BlockSpec((None, None, H_C, DK_C, DV_C), lambda b: (layer, b, 0, 0, 0)),
                full(cos), full(sin), full(dmask), full(cross), full(kdec), full(cdec), full(gain)]
    aliases = {}
    if st_all is not None:
        ins.append(st_all)
        in_specs.append(pl.BlockSpec(memory_space=pl.ANY))
        aliases = {len(ins) - 1: 1}
    return pl.pallas_call(
        _ret_sample_kernel,
        grid=(nb,),
        in_specs=in_specs,
        out_specs=[pl.BlockSpec((None, SUBLANES, H_C * DV_C), lambda b: (b, 0, 0)),
                   pl.BlockSpec((None, None, H_C, DK_C, DV_C), lambda b: (layer, b, 0, 0, 0))],
        out_shape=[jax.ShapeDtypeStruct((nb, SUBLANES, H_C * DV_C), BF16),
                   jax.ShapeDtypeStruct(state_c.shape, F32)],
        input_output_aliases=aliases,
        compiler_params=_cparams(("arbitrary",)),
        name="ret_sample",
    )(*ins)


ROUTE_LANE0 = N_GROUPS


def _route_kernel(x_ref, g_ref, sh_ref, sc_ref, wr_ref, br_ref, cin_ref,
                  h_ref, mi_ref, mf_ref, cnt_ref, carry_scr):
    i = pl.program_id(0)

    @pl.when(i == 0)
    def _():
        carry_scr[...] = cin_ref[...]

    h = (_rms(x_ref[...]) * g_ref[...]) * (1.0 + sc_ref[...]) + sh_ref[...]
    h_ref[...] = h
    logits = jnp.dot(h, wr_ref[...], preferred_element_type=F32,
                     precision=lax.Precision.HIGHEST) + br_ref[...]
    tm = h.shape[0]
    lane = lax.broadcasted_iota(I32, (tm, LANES), 1)
    big = jnp.int32(LANES)

    def first_max(v):
        m = jnp.max(v, axis=1, keepdims=True)
        return m, jnp.min(jnp.where(v == m, lane, big), axis=1, keepdims=True)

    is_g = lane < N_GROUPS
    mg, gsel = first_max(jnp.where(is_g, logits, NEG_INF))
    gate_g = 1.0 / jnp.sum(jnp.where(is_g, jnp.exp(logits - mg), 0.0), axis=1, keepdims=True)
    lo = ROUTE_LANE0 + gsel * E_PER_GROUP
    le = jnp.where((lane >= lo) & (lane < lo + E_PER_GROUP), logits, NEG_INF)
    m1, i1 = first_max(le)
    m2, i2 = first_max(jnp.where(lane == i1, NEG_INF, le))
    e2 = jnp.exp(m2 - m1)
    w1 = gate_g / (1.0 + e2)
    w2 = gate_g * e2 / (1.0 + e2)

    hit1 = lane == i1
    hit2 = lane == i2
    oh = (hit1 | hit2).astype(BF16)
    r = lax.broadcasted_iota(I32, (tm, tm), 0)
    c = lax.broadcasted_iota(I32, (tm, tm), 1)
    cum = jnp.dot((c < r).astype(BF16), oh, preferred_element_type=F32) + carry_scr[0:1, :]
    rank1 = jnp.sum(jnp.where(hit1, cum, 0.0), axis=1, keepdims=True).astype(I32)
    rank2 = jnp.sum(jnp.where(hit2, cum, 0.0), axis=1, keepdims=True).astype(I32)
    carry_scr[...] = carry_scr[...] + jnp.sum(oh.astype(F32), axis=0, keepdims=True)
    cnt_ref[...] = carry_scr[...]

    mi_ref[...] = jnp.where(lane == 0, i1 - ROUTE_LANE0,
                            jnp.where(lane == 1, i2 - ROUTE_LANE0,
                                      jnp.where(lane == 2, rank1, jnp.where(lane == 3, rank2, 0))))
    mf_ref[...] = jnp.where(lane == 0, w1, jnp.where(lane == 1, w2, 0.0))


def moe_route(x, gain, shift, scale, w_router, b_router, carry_in, rows_per_group, tm=256):
    m, d = x.shape
    tm = min(tm, m)
    steps = max(rows_per_group // tm, 1)
    shift, scale = _per_block(shift, tm), _per_block(scale, tm)
    mod = lambda a: pl.BlockSpec((None,) + a.shape[1:], lambda i: (i // steps, 0, 0))
    return pl.pallas_call(
        _route_kernel,
        grid=(m // tm,),
        in_specs=[pl.BlockSpec((tm, d), lambda i: (i, 0)),
                  pl.BlockSpec((1, d), lambda i: (0, 0)),
                  mod(shift), mod(scale),
                  pl.BlockSpec((d, LANES), lambda i: (0, 0)),
                  pl.BlockSpec((1, LANES), lambda i: (0, 0)),
                  pl.BlockSpec((SUBLANES, LANES), lambda i: (0, 0))],
        out_specs=[pl.BlockSpec((tm, d), lambda i: (i, 0)),
                   pl.BlockSpec((tm, LANES), lambda i: (i, 0)),
                   pl.BlockSpec((tm, LANES), lambda i: (i, 0)),
                   pl.BlockSpec((SUBLANES, LANES), lambda i: (0, 0))],
        out_shape=[jax.ShapeDtypeStruct((m, d), F32),
                   jax.ShapeDtypeStruct((m, LANES), I32),
                   jax.ShapeDtypeStruct((m, LANES), F32),
                   jax.ShapeDtypeStruct((SUBLANES, LANES), F32)],
        scratch_shapes=[pltpu.VMEM((SUBLANES, LANES), F32)],
        compiler_params=_cparams(("arbitrary",)),
        name="moe_route",
    )(x, gain.reshape(1, d), shift, scale, w_router, b_router, carry_in)


def _dispatch_kernel(pos_ref, h_ref, xs_in_ref, xs_ref, sem, *, tb):
    del xs_in_ref
    base = pl.program_id(0) * tb

    def issue(t, carry):
        tok = base + t
        for j in range(2):
            pltpu.make_async_copy(h_ref.at[pl.ds(tok, 1)],
                                  xs_ref.at[pl.ds(pos_ref[2 * tok + j], 1)], sem).start()
        return carry

    lax.fori_loop(0, tb, issue, 0)

    def drain(t, carry):
        for j in range(2):
            pltpu.make_async_copy(h_ref.at[pl.ds(0, 1)], xs_ref.at[pl.ds(0, 1)], sem).wait()
        return carry

    lax.fori_loop(0, tb, drain, 0)


def moe_dispatch(pos_flat, h, n_slots, tb=256):
    n, d = h.shape
    grid_spec = pltpu.PrefetchScalarGridSpec(
        num_scalar_prefetch=1,
        grid=(n // tb,),
        in_specs=[pl.BlockSpec(memory_space=pl.ANY), pl.BlockSpec(memory_space=pl.ANY)],
        out_specs=pl.BlockSpec(memory_space=pl.ANY),
        scratch_shapes=[pltpu.SemaphoreType.DMA(())],
    )
    return pl.pallas_call(
        functools.partial(_dispatch_kernel, tb=tb), grid_spec=grid_spec,
        out_shape=jax.ShapeDtypeStruct((n_slots, d), F32),
        input_output_aliases={2: 0},
        compiler_params=_cparams(("arbitrary",)),
        name="moe_dispatch",
    )(pos_flat, h, jnp.zeros((n_slots, d), F32))


def _expert_kernel(te_ref, x_ref, w1_ref, w3_ref, w2_ref, y_ref):
    i = pl.program_id(0)

    @pl.when(te_ref[1, i] > 0)
    def _():
        x = x_ref[...].astype(BF16)
        a = jnp.dot(x, w1_ref[...].astype(BF16), preferred_element_type=F32)
        b = jnp.dot(x, w3_ref[...].astype(BF16), preferred_element_type=F32)
        act = (a * jax.nn.sigmoid(a)) * b
        y_ref[...] = _bdot(act, w2_ref[...])

    @pl.when(te_ref[1, i] == 0)
    def _():
        y_ref[...] = jnp.zeros_like(y_ref)


def moe_experts(tile_info, xs, w1, w3, w2, layer, tm):
    n_slots, d = xs.shape
    f = w1.shape[3]
    grid_spec = pltpu.PrefetchScalarGridSpec(
        num_scalar_prefetch=1,
        grid=(n_slots // tm,),
        in_specs=[pl.BlockSpec((tm, d), lambda i, te: (i, 0)),
                  pl.BlockSpec((None, None, d, f), lambda i, te: (layer, te[0, i], 0, 0)),
                  pl.BlockSpec((None, None, d, f), lambda i, te: (layer, te[0, i], 0, 0)),
                  pl.BlockSpec((None, None, f, d), lambda i, te: (layer, te[0, i], 0, 0))],
        out_specs=pl.BlockSpec((tm, d), lambda i, te: (i, 0)),
    )
    return pl.pallas_call(
        _expert_kernel, grid_spec=grid_spec,
        out_shape=jax.ShapeDtypeStruct((n_slots, d), F32),
        compiler_params=_cparams(("arbitrary",)),
        name="moe_experts",
    )(tile_info, xs, w1, w3, w2)


def _combine_kernel(pos_ref, x_ref, gate_ref, mf_ref, ys_ref, o_ref, buf, sem, *, tb, tok0):
    base = tok0 + pl.program_id(0) * tb

    def issue(t, carry):
        for j in range(2):
            pltpu.make_async_copy(ys_ref.at[pl.ds(pos_ref[2 * (base + t) + j], 1)],
                                  buf.at[j, pl.ds(t, 1)], sem).start()
        return carry

    lax.fori_loop(0, tb, issue, 0)

    def drain(t, carry):
        for j in range(2):
            pltpu.make_async_copy(ys_ref.at[pl.ds(0, 1)], buf.at[j, pl.ds(0, 1)], sem).wait()
        return carry

    lax.fori_loop(0, tb, drain, 0)
    y = mf_ref[:, 0:1] * buf[0] + mf_ref[:, 1:2] * buf[1]
    o_ref[...] = x_ref[...] + gate_ref[...] * y


def moe_combine(pos_flat, x, gate, mf, ys, tok0, rows_per_group, tb=256):
    m, d = x.shape
    tb = min(tb, m)
    steps = max(rows_per_group // tb, 1)
    gate = _per_block(gate, tb)
    grid_spec = pltpu.PrefetchScalarGridSpec(
        num_scalar_prefetch=1,
        grid=(m // tb,),
        in_specs=[pl.BlockSpec((tb, d), lambda i, p: (i, 0)),
                  pl.BlockSpec((None,) + gate.shape[1:], lambda i, p: (i // steps, 0, 0)),
                  pl.BlockSpec((tb, LANES), lambda i, p: (i, 0)),
                  pl.BlockSpec(memory_space=pl.ANY)],
        out_specs=pl.BlockSpec((tb, d), lambda i, p: (i, 0)),
        scratch_shapes=[pltpu.VMEM((2, tb, d), F32), pltpu.SemaphoreType.DMA(())],
    )
    return pl.pallas_call(
        functools.partial(_combine_kernel, tb=tb, tok0=tok0), grid_spec=grid_spec,
        out_shape=jax.ShapeDtypeStruct((m, d), F32),
        compiler_params=_cparams(("arbitrary",)),
        name="moe_combine",
    )(pos_flat, x, gate, mf, ys)


def _final_norm_kernel(x_ref, g_ref, o_ref):
    o_ref[...] = _rms(x_ref[...]) * g_ref[...]


def final_norm(x, gain, tm=512):
    m, d = x.shape
    tm = min(tm, m)
    return pl.pallas_call(
        _final_norm_kernel,
        grid=(m // tm,),
        in_specs=[pl.BlockSpec((tm, d), lambda i: (i, 0)), pl.BlockSpec((1, d), lambda i: (0, 0))],
        out_specs=pl.BlockSpec((tm, d), lambda i: (i, 0)),
        out_shape=jax.ShapeDtypeStruct((m, d), F32),
        compiler_params=_cparams(("arbitrary",)),
        name="final_norm",
    )(x, gain.reshape(1, d))


def _t5_bucket(dist):
    max_exact = N_BUCKETS // 2
    n = jnp.maximum(dist, 0)
    large = max_exact + (jnp.log(jnp.maximum(n, 1).astype(F32) / max_exact)
                         / math.log(MAX_DISTANCE / max_exact) * (N_BUCKETS - max_exact)).astype(I32)
    large = jnp.minimum(large, N_BUCKETS - 1)
    return jnp.where(n < max_exact, n, large)


def _rope_cos_sin(pos, half):
    inv = ROPE_BASE ** (-jnp.arange(half, dtype=F32) / half)
    ang = pos.astype(F32)[:, None] * inv[None, :]
    return jnp.cos(ang), jnp.sin(ang)


def _even_rope_tables(pos):
    cos, sin = _rope_cos_sin(pos, ROPE_B // 2)
    c64 = jnp.concatenate([cos, cos], axis=1)
    s64 = jnp.concatenate([-sin, sin], axis=1)
    one = jnp.ones_like(c64)
    zero = jnp.zeros_like(s64)
    return (jnp.tile(c64, (1, H_B)), jnp.tile(s64, (1, H_B)),
            jnp.concatenate([one, c64], axis=1), jnp.concatenate([zero, s64], axis=1))


def _retention_tables(pos, c):
    log_g = jnp.log1p(-(2.0 ** (-5.0 - jnp.arange(H_C, dtype=F32))))
    i = jnp.arange(c, dtype=F32)
    diff = i[:, None] - i[None, :]
    dmask = jnp.where(diff >= 0, jnp.exp(jnp.maximum(diff, 0.0)[None] * log_g[:, None, None]), 0.0)
    cross = jnp.exp((i[:, None] + 1.0) * log_g[None, :])
    kdec = jnp.exp((c - 1.0 - i)[:, None] * log_g[None, :])
    chunk_dec = jnp.exp(c * log_g)
    cos, sin = _rope_cos_sin(pos, DK_C // 2)
    return dmask, cross.T[:, :, None], kdec.T[:, :, None], chunk_dec[:, None, None], cos, sin


def _pad_rows(a, axis, n):
    pad = [(0, 0)] * a.ndim
    pad[axis] = (0, n - a.shape[axis])
    return jnp.pad(a, pad)


def _even_weight_layout(w_in):
    qa, ka, va, iq, ik, iw, cq, ckv, kpe = jnp.split(
        w_in, np.cumsum([1024, 256, 256, 1024, 64, 16, 512, 256, 64])[:-1].tolist(), axis=1)
    pad = jnp.zeros((w_in.shape[0], EV_COLS - EV_IW - H_I), w_in.dtype)
    return jnp.concatenate([qa, iq, cq, ka, va, ckv, ik, kpe, iw, pad], axis=1)


def _moe_block(xp, xs, mods_p, mods_s, gain, w_router, b_router, w1, w3, w2, layer, seq, tm_e=256):
    sh_p, sc_p, gt_p = mods_p
    sh_s, sc_s, gt_s = mods_s
    n_p, n_s = xp.shape[0], xs.shape[0]
    zero_carry = jnp.zeros((SUBLANES, LANES), F32)
    h_p, mi_p, mf_p, cnt_p = moe_route(xp, gain, sh_p, sc_p, w_router, b_router, zero_carry, seq)
    h_s, mi_s, mf_s, cnt = moe_route(xs, gain, sh_s, sc_s, w_router, b_router, cnt_p, 1)
    h = jnp.concatenate([h_p, h_s], axis=0)
    mi = jnp.concatenate([mi_p, mi_s], axis=0)
    eid, rank = mi[:, 0:2], mi[:, 2:4]
    counts = cnt[0, ROUTE_LANE0:ROUTE_LANE0 + N_EXPERTS].astype(I32)
    padded = ((counts + tm_e - 1) // tm_e) * tm_e
    ends = jnp.cumsum(padded)
    off = ends - padded
    pos = (off[eid] + rank).reshape(-1).astype(I32)
    n_tiles = (2 * (n_p + n_s)) // tm_e + N_EXPERTS
    starts = jnp.arange(n_tiles, dtype=I32) * tm_e
    tile_e = jnp.searchsorted(ends, starts, side="right").astype(I32)
    valid = (starts < ends[-1]).astype(I32)
    last_e = jnp.max(jnp.where(counts > 0, jnp.arange(N_EXPERTS, dtype=I32), 0))
    tile_e = jnp.where(valid > 0, jnp.minimum(tile_e, N_EXPERTS - 1), last_e)
    tile_info = jnp.stack([tile_e, valid])
    xs_sorted = moe_dispatch(pos, h, n_tiles * tm_e)
    ys = moe_experts(tile_info, xs_sorted, w1, w3, w2, layer, tm_e)
    xp = moe_combine(pos, xp, gt_p, mf_p, ys, 0, seq)
    xs = moe_combine(pos, xs, gt_s, mf_s, ys, n_p, 1)
    return xp, xs


def kernel(x_prompt, x_sample, cache_a_k, cache_a_v, cache_a_idx, cache_b_latent, cache_b_rope, state_c,
           page_table, c_prompt, c_sample, rel_bias, w_ada, b_ada, norm_mix, norm_ffn, norm_final,
           w_in_even, q_norm_b, kv_norm_b, w_uq_b, w_uk_b, w_uv_b, w_out_even, w_in_odd, gn_gain_c,
           w_out_odd, w_rg, b_rg, w_re, b_re, w1, w3, w2):
    nb, seq, d = x_prompt.shape
    db, ds, _ = x_sample.shape
    n_p, n_s = nb * seq, db * ds
    past = page_table.shape[1] * PAGE_SIZE
    topk_p = min(TOPK_MAX, seq // 4)
    topk_s = min(TOPK_MAX, (past + ds) // 4)
    pos_p = jnp.arange(seq, dtype=I32)
    pos_s = past + jnp.arange(ds, dtype=I32)

    n_seq = nb + db
    r_pad = -(-n_seq // SUBLANES) * SUBLANES
    c_all = _pad_rows(jnp.concatenate([c_prompt, c_sample], axis=0), 0, r_pad)
    ada = ada_all(c_all, w_ada, b_ada)

    def mods(l):
        mp = ada[l, :nb].reshape(nb, 6, 1, d)
        ms = ada[l, nb:n_seq].reshape(db, 6, d)
        ms = jnp.repeat(ms[:, :, None, :], ds, axis=2)
        ms = jnp.moveaxis(ms, 1, 0).reshape(6, 1, n_s, d)
        return [mp[:, j] for j in range(6)], [ms[j] for j in range(6)]

    rb = rel_bias - rel_bias[N_BUCKETS - 1][None, :]
    ii = jnp.arange(QBLK, dtype=I32)[:, None]
    cc = jnp.arange(2 * QBLK, dtype=I32)[None, :]
    tz_p = jnp.moveaxis(rb[_t5_bucket(ii - cc + QBLK)], -1, 0)
    tt = jnp.minimum(jnp.arange(SUBLANES, dtype=I32), ds - 1)[:, None]
    near0 = past + PAGE_SIZE - 2 * PAGE_SIZE
    tz_s = jnp.moveaxis(rb[_t5_bucket(past + tt - (near0 + cc))], -1, 0)
    grp = H_A // KV_A
    tz_s = tz_s.reshape(KV_A, grp * SUBLANES, 2 * PAGE_SIZE)

    ev_tabs_p = _even_rope_tables(pos_p)
    ev_tabs_s = tuple(jnp.tile(t, (db, 1)) for t in _even_rope_tables(pos_s))
    ch_p = min(RET_CHUNK, seq)
    dmask_p, cross_p, kdec_p, cdec_p, cos_p, sin_p = _retention_tables(pos_p, ch_p)
    ret_tabs_p = (cos_p, sin_p, dmask_p, cross_p, kdec_p, cdec_p)
    dmask_s, cross_s, kdec_s, cdec_s, cos_s, sin_s = _retention_tables(pos_s, ds)
    ret_tabs_s = (_pad_rows(cos_s, 0, SUBLANES), _pad_rows(sin_s, 0, SUBLANES),
                  _pad_rows(_pad_rows(dmask_s, 1, SUBLANES), 2, SUBLANES),
                  _pad_rows(cross_s, 1, SUBLANES), _pad_rows(kdec_s, 1, SUBLANES), cdec_s)

    ck = cache_a_k.reshape(cache_a_k.shape[:3] + (KV_A * HD_A,))
    cv = cache_a_v.reshape(cache_a_v.shape[:3] + (KV_A * HD_A,))

    xp = x_prompt.reshape(n_p, d)
    xs = x_sample.reshape(n_s, d)
    outs_p = {k: [] for k in ("k", "v", "idx", "lat", "rope", "st")}
    outs_s = {k: [] for k in ("k", "v", "idx", "lat", "rope")}
    st_s_all = None

    def head_tok_rows(a, heads, width):
        a = a.reshape(db, ds, heads, width).transpose(0, 2, 1, 3)
        return _pad_rows(a, 2, SUBLANES).reshape(db, heads * SUBLANES, width)

    def new_page(a):
        return _pad_rows(a.reshape(db, ds, a.shape[-1]), 1, PAGE_SIZE)

    for l in range(DEPTH):
        mp, ms = mods(l)
        if l % 2 == 0:
            e = l // 2
            w_in = _even_weight_layout(w_in_even[e])
            qn, kvn = q_norm_b[e].reshape(1, -1), kv_norm_b[e].reshape(1, -1)
            wuq = w_uq_b[e].reshape(Q_LORA, H_B, NOPE_B + ROPE_B)
            wuq_r = jnp.concatenate([wuq[:, :, :NOPE_B].reshape(Q_LORA, -1),
                                     wuq[:, :, NOPE_B:].reshape(Q_LORA, -1)], axis=1)
            wuk_t = jnp.transpose(w_uk_b[e], (1, 2, 0))
            wuv2 = w_uv_b[e].reshape(KV_LORA, H_B * V_B)
            wuv_h = jnp.transpose(w_uv_b[e], (1, 0, 2))

            proj = nm_matmul(xp, norm_mix[l], mp[0], mp[1], w_in, seq)
            qlat, qpe, ckvn, ikkpe = even_post(proj, qn, kvn, wuq_r, wuk_t, ev_tabs_p)
            mix = prompt_mix(proj, qlat, qpe, ckvn, ikkpe, tz_p, wuv2, nb, seq, topk_p)
            xp = mm_residual(mix, w_out_even[e], xp, mp[2], seq)
            outs_p["k"].append(proj[:, EV_KA:EV_KA + 256].reshape(nb, seq, KV_A, HD_A))
            outs_p["v"].append(proj[:, EV_VA:EV_VA + 256].reshape(nb, seq, KV_A, HD_A))
            outs_p["idx"].append(ikkpe[:, :D_I].reshape(nb, seq, D_I))
            outs_p["lat"].append(ckvn.reshape(nb, seq, KV_LORA))
            outs_p["rope"].append(ikkpe[:, D_I:].reshape(nb, seq, ROPE_B))

            proj_s = nm_matmul(xs, norm_mix[l], ms[0], ms[1], w_in, 1)
            qlat_s, qpe_s, ckvn_s, ikkpe_s = even_post(proj_s, qn, kvn, wuq_r, wuk_t, ev_tabs_s)
            ka_s = proj_s[:, EV_KA:EV_KA + 256]
            va_s = proj_s[:, EV_VA:EV_VA + 256]
            ik_s, kpe_s = ikkpe_s[:, :D_I], ikkpe_s[:, D_I:]
            q_idx = head_tok_rows(proj_s[:, EV_IQ:EV_IQ + H_I * D_I], H_I, D_I)
            w_idx = head_tok_rows(proj_s[:, EV_IW:EV_IW + H_I] * (H_I ** -0.5 * D_I ** -0.5), H_I, 1)
            mask = sample_select(page_table, q_idx, w_idx, cache_a_idx, e, new_page(ik_s), ds, topk_s)
            q_a = head_tok_rows(proj_s[:, EV_QA:EV_QA + H_A * HD_A], H_A, HD_A)
            q_a = q_a.reshape(db, KV_A, grp * SUBLANES, HD_A)
            oa = sample_dsa(page_table, q_a, mask, tz_s, ck, cv, e, new_page(ka_s), new_page(va_s))
            oa = oa.reshape(db, H_A, SUBLANES, HD_A)[:, :, :ds].transpose(0, 2, 1, 3).reshape(n_s, -1)
            o_lat = sample_mla(page_table, head_tok_rows(qlat_s, H_B, KV_LORA),
                               head_tok_rows(qpe_s, H_B, ROPE_B), cache_b_latent, cache_b_rope, e,
                               new_page(ckvn_s), new_page(kpe_s), ds)
            o_lat = o_lat.reshape(db, H_B, SUBLANES, KV_LORA)[:, :, :ds]
            o_lat = o_lat.transpose(1, 0, 2, 3).reshape(H_B, n_s, KV_LORA)
            ob = head_proj(o_lat, wuv_h).transpose(1, 0, 2).reshape(n_s, -1)
            mix_s = jnp.concatenate([oa, ob], axis=1)
            xs = mm_residual(mix_s, w_out_even[e], xs, ms[2], 1)
            outs_s["k"].append(ka_s.reshape(db, ds, KV_A, HD_A))
            outs_s["v"].append(va_s.reshape(db, ds, KV_A, HD_A))
            outs_s["idx"].append(ik_s.reshape(db, ds, D_I))
            outs_s["lat"].append(ckvn_s.reshape(db, ds, KV_LORA))
            outs_s["rope"].append(kpe_s.reshape(db, ds, ROPE_B))
        else:
            o = l // 2
            gain = gn_gain_c[o].reshape(1, -1)
            proj = nm_matmul(xp, norm_mix[l], mp[0], mp[1], w_in_odd[o], seq)
            og, st_p = ret_prompt(proj, ret_tabs_p, gain, nb, seq)
            xp = mm_residual(og, w_out_odd[o], xp, mp[2], seq)
            outs_p["st"].append(st_p)
            proj_s = nm_matmul(xs, norm_mix[l], ms[0], ms[1], w_in_odd[o], 1)
            proj_s = _pad_rows(proj_s.reshape(db, ds, -1), 1, SUBLANES)
            og_s, st_s_all = ret_sample(proj_s, state_c, o, ret_tabs_s, gain, st_s_all)
            og_s = og_s[:, :ds].reshape(n_s, -1)
            xs = mm_residual(og_s, w_out_odd[o], xs, ms[2], 1)

        w_router = jnp.concatenate(
            [w_rg[l], w_re[l], jnp.zeros((d, LANES - N_GROUPS - N_EXPERTS), F32)], axis=1)
        b_router = jnp.concatenate(
            [b_rg[l], b_re[l], jnp.zeros((LANES - N_GROUPS - N_EXPERTS,), F32)]).reshape(1, LANES)
        xp, xs = _moe_block(xp, xs, (mp[3], mp[4], mp[5]), (ms[3], ms[4], ms[5]), norm_ffn[l],
                            w_router, b_router, w1, w3, w2, l, seq)

    y_p = final_norm(xp, norm_final).reshape(nb, seq, d)
    y_s = final_norm(xs, norm_final).reshape(db, ds, d)
    return (y_p, y_s,
            jnp.stack(outs_p["k"]), jnp.stack(outs_p["v"]), jnp.stack(outs_p["idx"]),
            jnp.stack(outs_p["lat"]), jnp.stack(outs_p["rope"]), jnp.stack(outs_p["st"]),
            jnp.stack(outs_s["k"]), jnp.stack(outs_s["v"]), jnp.stack(outs_s["idx"]),
            jnp.stack(outs_s["lat"]), jnp.stack(outs_s["rope"]), st_s_all)
```

```python
import functools
import math

import numpy as np
import jax
import jax.numpy as jnp
from jax import lax
from jax.experimental import pallas as pl
from jax.experimental.pallas import tpu as pltpu

F32 = jnp.float32
BF16 = jnp.bfloat16
I32 = jnp.int32

D_MODEL = 2048
DEPTH = 4
PAST_LEN = 8192
PAGE_SIZE = 128
H_A, HD_A, KV_A = 8, 128, 2
H_I, D_I = 16, 64
TOPK_MAX = 256
N_BUCKETS, MAX_DISTANCE = 32, 128
H_B, Q_LORA, KV_LORA, NOPE_B, ROPE_B, V_B = 8, 512, 256, 128, 64, 128
ROPE_BASE = 10000.0
H_C = 8
DK_C = D_MODEL // H_C
DV_C = 2 * D_MODEL // H_C
RET_CHUNK = 128
N_GROUPS, E_PER_GROUP = 4, 8
N_EXPERTS = N_GROUPS * E_PER_GROUP
D_EXPERT = 512
EPS = 1e-6

LANES = 128
SUBLANES = 8
VMEM_LIMIT = 52 * 1024 * 1024

EV_QA, EV_IQ, EV_CQ, EV_KA, EV_VA, EV_CKV, EV_IK, EV_KPE, EV_IW = (
    0, 1024, 2048, 2560, 2816, 3072, 3328, 3392, 3456)
EV_COLS = 3584
QBLK = 128
NEG_INF = float("-inf")


def _cparams(sem, vmem=VMEM_LIMIT):
    return pltpu.CompilerParams(dimension_semantics=sem, vmem_limit_bytes=vmem)


def _bdot(a, b):
    return jnp.dot(a.astype(BF16), b.astype(BF16), preferred_element_type=F32)


def _bdot_nt(a, b):
    return lax.dot_general(a.astype(BF16), b.astype(BF16), (((1,), (1,)), ((), ())),
                           preferred_element_type=F32)


def _bdot_tn(a, b):
    return lax.dot_general(a.astype(BF16), b.astype(BF16), (((0,), (0,)), ((), ())),
                           preferred_element_type=F32)


def _rms(x):
    return x * lax.rsqrt(jnp.mean(x * x, axis=-1, keepdims=True) + EPS)


def _ada_kernel(c_ref, w_ref, b_ref, o_ref):
    c = c_ref[...]
    a = c * jax.nn.sigmoid(c)
    o_ref[...] = _bdot(a, w_ref[...]) + b_ref[...]


def ada_all(c, w_ada, b_ada):
    r, d = c.shape
    nl, _, n = w_ada.shape
    tn = 1024
    return pl.pallas_call(
        _ada_kernel,
        grid=(nl, n // tn),
        in_specs=[pl.BlockSpec((r, d), lambda l, j: (0, 0)),
                  pl.BlockSpec((None, d, tn), lambda l, j: (l, 0, j)),
                  pl.BlockSpec((None, 1, tn), lambda l, j: (l, 0, j))],
        out_specs=pl.BlockSpec((None, r, tn), lambda l, j: (l, 0, j)),
        out_shape=jax.ShapeDtypeStruct((nl, r, n), F32),
        compiler_params=_cparams(("arbitrary", "arbitrary")),
        name="ada_all",
    )(c, w_ada, b_ada.reshape(nl, 1, n))


def _nm_matmul_kernel(x_ref, g_ref, sh_ref, sc_ref, w_ref, o_ref, h_scr):
    @pl.when(pl.program_id(1) == 0)
    def _():
        rows = min(256, x_ref.shape[0])
        per_token = sh_ref.shape[0] != 1

        def body(r, carry):
            sl = pl.ds(pl.multiple_of(r * rows, rows), rows)
            sc = sc_ref[sl, :] if per_token else sc_ref[...]
            sh = sh_ref[sl, :] if per_token else sh_ref[...]
            h = (_rms(x_ref[sl, :]) * g_ref[...]) * (1.0 + sc) + sh
            h_scr[sl, :] = h.astype(BF16)
            return carry

        lax.fori_loop(0, x_ref.shape[0] // rows, body, 0)

    o_ref[...] = jnp.dot(h_scr[...], w_ref[...].astype(BF16), preferred_element_type=F32)


def _per_block(mod, tm):
    if mod.shape[1] == 1:
        return mod
    return mod.reshape(-1, tm, mod.shape[-1])


def _mod_spec(mod, tm, rows_per_group):
    steps = max(rows_per_group // tm, 1)
    return pl.BlockSpec((None,) + mod.shape[1:], lambda i, j: (i // steps, 0, 0))


def nm_matmul(x, gain, shift, scale, w, rows_per_group, tm=2048, tn=512):
    m, d = x.shape
    n = w.shape[1]
    tm = min(tm, m)
    shift, scale = _per_block(shift, tm), _per_block(scale, tm)
    return pl.pallas_call(
        _nm_matmul_kernel,
        grid=(m // tm, n // tn),
        in_specs=[pl.BlockSpec((tm, d), lambda i, j: (i, 0), pipeline_mode=pl.Buffered(1)),
                  pl.BlockSpec((1, d), lambda i, j: (0, 0)),
                  _mod_spec(shift, tm, rows_per_group),
                  _mod_spec(scale, tm, rows_per_group),
                  pl.BlockSpec((d, tn), lambda i, j: (0, j))],
        out_specs=pl.BlockSpec((tm, tn), lambda i, j: (i, j)),
        out_shape=jax.ShapeDtypeStruct((m, n), F32),
        scratch_shapes=[pltpu.VMEM((tm, d), BF16)],
        compiler_params=_cparams(("arbitrary", "arbitrary")),
        name="nm_matmul",
    )(x, gain.reshape(1, d), shift, scale, w)


def _mm_res_kernel(a_ref, w_ref, x_ref, gate_ref, o_ref):
    o_ref[...] = x_ref[...] + gate_ref[...] * _bdot(a_ref[...], w_ref[...])


def mm_residual(a, w, x, gate, rows_per_group, tm=1024, tn=512):
    m, k = a.shape
    n = w.shape[1]
    tm = min(tm, m)
    steps = max(rows_per_group // tm, 1)
    gate = _per_block(gate, tm)
    return pl.pallas_call(
        _mm_res_kernel,
        grid=(m // tm, n // tn),
        in_specs=[pl.BlockSpec((tm, k), lambda i, j: (i, 0)),
                  pl.BlockSpec((k, tn), lambda i, j: (0, j)),
                  pl.BlockSpec((tm, tn), lambda i, j: (i, j)),
                  pl.BlockSpec((None, gate.shape[1], tn), lambda i, j: (i // steps, 0, j))],
        out_specs=pl.BlockSpec((tm, tn), lambda i, j: (i, j)),
        out_shape=jax.ShapeDtypeStruct((m, n), F32),
        compiler_params=_cparams(("arbitrary", "arbitrary")),
        name="mm_residual",
    )(a, w, x, gate)


def _swap_halves(x, half):
    n = x.shape[-1]
    lane = lax.broadcasted_iota(I32, x.shape, x.ndim - 1)
    return jnp.where(lane % (2 * half) < half,
                     pltpu.roll(x, n - half, x.ndim - 1), pltpu.roll(x, half, x.ndim - 1))


def _even_post_kernel(cq_ref, ckv_ref, slab_ref, qn_ref, kvn_ref, wuq_ref, wuk_ref,
                      cq_c_ref, cq_s_ref, cs_c_ref, cs_s_ref,
                      qlat_ref, qpe_ref, ckvn_ref, ikkpe_ref):
    cqn = _rms(cq_ref[...]) * qn_ref[...]
    qb = _bdot(cqn, wuq_ref[...])
    nope = H_B * NOPE_B
    for h in range(H_B):
        qlat_ref[:, h * KV_LORA:(h + 1) * KV_LORA] = _bdot(
            qb[:, h * NOPE_B:(h + 1) * NOPE_B], wuk_ref[h])
    qr = qb[:, nope:]
    qpe_ref[...] = qr * cq_c_ref[...] + _swap_halves(qr, ROPE_B // 2) * cq_s_ref[...]
    ckvn_ref[...] = _rms(ckv_ref[...]) * kvn_ref[...]
    slab = slab_ref[:, :LANES]
    ikkpe_ref[...] = slab * cs_c_ref[...] + _swap_halves(slab, ROPE_B // 2) * cs_s_ref[...]


def even_post(proj, q_norm, kv_norm, w_uq_r, w_uk_t, tabs, tm=256):
    m = proj.shape[0]
    tm = min(tm, m)
    cq_c, cq_s, cs_c, cs_s = tabs
    nt = cq_c.shape[0] // tm
    tab = lambda w: pl.BlockSpec((tm, w), lambda i: (i % nt, 0))
    full = lambda a: pl.BlockSpec(a.shape, lambda i: (0,) * a.ndim)
    return pl.pallas_call(
        _even_post_kernel,
        grid=(m // tm,),
        in_specs=[pl.BlockSpec((tm, Q_LORA), lambda i: (i, EV_CQ // Q_LORA)),
                  pl.BlockSpec((tm, KV_LORA), lambda i: (i, EV_CKV // KV_LORA)),
                  pl.BlockSpec((tm, 256), lambda i: (i, EV_IK // 256)),
                  full(q_norm), full(kv_norm), full(w_uq_r), full(w_uk_t),
                  tab(512), tab(512), tab(LANES), tab(LANES)],
        out_specs=[pl.BlockSpec((tm, H_B * KV_LORA), lambda i: (i, 0)),
                   pl.BlockSpec((tm, H_B * ROPE_B), lambda i: (i, 0)),
                   pl.BlockSpec((tm, KV_LORA), lambda i: (i, 0)),
                   pl.BlockSpec((tm, LANES), lambda i: (i, 0))],
        out_shape=[jax.ShapeDtypeStruct((m, H_B * KV_LORA), F32),
                   jax.ShapeDtypeStruct((m, H_B * ROPE_B), F32),
                   jax.ShapeDtypeStruct((m, KV_LORA), F32),
                   jax.ShapeDtypeStruct((m, LANES), F32)],
        compiler_params=_cparams(("arbitrary",)),
        name="even_post",
    )(proj, proj, proj, q_norm, kv_norm, w_uq_r, w_uk_t, cq_c, cq_s, cs_c, cs_s)


def _topk_mask(score, valid, kpos, topk, idx_bits):
    sm = jnp.where(valid, jnp.where(score == 0.0, 0.0, score), NEG_INF)
    bits = pltpu.bitcast(sm, I32)
    key = jnp.where(bits < 0, bits ^ jnp.int32(0x7FFFFFFF), bits)

    def count(m):
        return jnp.sum(m.astype(I32), axis=1, keepdims=True)

    int_min = jnp.int32(-2 ** 31)
    t0 = jnp.where(count(key >= 0) >= topk, jnp.int32(0), int_min)

    def bit_step(i, t):
        cand = t | (jnp.int32(1) << (30 - i))
        return jnp.where(count(key >= cand) >= topk, cand, t)

    thr = lax.fori_loop(0, 31, bit_step, jnp.broadcast_to(t0, (score.shape[0], 1)))
    above = key > thr
    ties = key == thr
    need = topk - count(above)

    def idx_step(i, j):
        cand = j | (jnp.int32(1) << (idx_bits - 1 - i))
        return jnp.where(count(ties & (kpos < cand)) < need, cand, j)

    last = lax.fori_loop(0, idx_bits, idx_step, jnp.zeros((score.shape[0], 1), I32))
    return above | (ties & (kpos <= last))


def _softmax_rows(s):
    p = jnp.exp(s - jnp.max(s, axis=1, keepdims=True))
    return p, 1.0 / jnp.sum(p, axis=1, keepdims=True)


def _prompt_mix_body(qa_ref, iq_ref, qslab_ref, qlat_ref, qpe_ref, ka_ref, va_ref, ckv_ref,
                     ikkpe_ref, tz_ref, wuv_ref, o_ref, s_scr, *, nk, topk, idx_bits):
    qi = pl.program_id(1)
    tq = qa_ref.shape[0]
    q0 = pl.multiple_of(qi * tq, tq)
    qpos = q0 + lax.broadcasted_iota(I32, (tq, nk), 0)
    kpos = lax.broadcasted_iota(I32, (tq, nk), 1)
    causal = kpos <= qpos

    ikb = ikkpe_ref[:nk, :D_I].astype(BF16)
    kpeb = ikkpe_ref[:nk, D_I:].astype(BF16)
    wq = qslab_ref[:, LANES:LANES + H_I] * (H_I ** -0.5 * D_I ** -0.5)
    score = jnp.zeros((tq, nk), F32)
    for h in range(H_I):
        s = _bdot_nt(iq_ref[:, h * D_I:(h + 1) * D_I], ikb)
        score = score + wq[:, h:h + 1] * jnp.maximum(s, 0.0)
    sel = _topk_mask(score, causal, kpos, topk, idx_bits) & causal

    grp = H_A // KV_A
    for kv in range(KV_A):
        kk = ka_ref[:nk, kv * HD_A:(kv + 1) * HD_A].astype(BF16)
        vv = va_ref[:nk, kv * HD_A:(kv + 1) * HD_A].astype(BF16)
        for g in range(grp):
            h = kv * grp + g
            s_scr[:, :nk] = _bdot_nt(qa_ref[:, h * HD_A:(h + 1) * HD_A], kk) * (HD_A ** -0.5)
            s_scr[:, pl.ds(q0, tq)] += tz_ref[h, :, tq:]

            @pl.when(qi > 0)
            def _():
                s_scr[:, pl.ds(q0 - tq, tq)] += tz_ref[h, :, :tq]

            p, rl = _softmax_rows(jnp.where(sel, s_scr[:, :nk], NEG_INF))
            o_ref[:, h * HD_A:(h + 1) * HD_A] = (_bdot(p, vv) * rl).astype(o_ref.dtype)

    ckvb = ckv_ref[:nk, :].astype(BF16)
    off = H_A * HD_A
    for h in range(H_B):
        s = (_bdot_nt(qlat_ref[:, h * KV_LORA:(h + 1) * KV_LORA], ckvb)
             + _bdot_nt(qpe_ref[:, h * ROPE_B:(h + 1) * ROPE_B], kpeb)) * ((NOPE_B + ROPE_B) ** -0.5)
        p, rl = _softmax_rows(jnp.where(causal, s, NEG_INF))
        o_lat = _bdot(p, ckvb) * rl
        o_ref[:, off + h * V_B:off + (h + 1) * V_B] = _bdot(
            o_lat, wuv_ref[:, h * V_B:(h + 1) * V_B]).astype(o_ref.dtype)


def _prompt_mix_kernel(*refs, key_step, topk, idx_bits):
    qi = pl.program_id(1)
    tq = refs[0].shape[0]
    seq = refs[5].shape[0]
    for v in range(seq // key_step):
        @pl.when(qi // (key_step // tq) == v)
        def _():
            _prompt_mix_body(*refs, nk=(v + 1) * key_step, topk=topk, idx_bits=idx_bits)


def prompt_mix(proj, qlat, qpe, ckvn, ikkpe, tz, w_uv2, nb, seq, topk):
    tq = QBLK
    nq = seq // tq
    row = lambda w, c: pl.BlockSpec((tq, w), lambda b, q: (b * nq + q, c))
    keys = lambda w, c: pl.BlockSpec((seq, w), lambda b, q: (b, c))
    full = lambda a: pl.BlockSpec(a.shape, lambda b, q: (0,) * a.ndim)
    kern = functools.partial(_prompt_mix_kernel, key_step=min(4 * tq, seq), topk=topk,
                             idx_bits=max(int(math.ceil(math.log2(seq))), 1))
    return pl.pallas_call(
        kern,
        grid=(nb, nq),
        in_specs=[row(1024, EV_QA // 1024), row(1024, EV_IQ // 1024), row(256, EV_IK // 256),
                  row(H_B * KV_LORA, 0), row(H_B * ROPE_B, 0),
                  keys(256, EV_KA // 256), keys(256, EV_VA // 256),
                  keys(KV_LORA, 0), keys(LANES, 0), full(tz), full(w_uv2)],
        out_specs=pl.BlockSpec((tq, 2048), lambda b, q: (b * nq + q, 0)),
        out_shape=jax.ShapeDtypeStruct((nb * seq, 2048), BF16),
        scratch_shapes=[pltpu.VMEM((tq, seq), F32)],
        compiler_params=_cparams(("arbitrary", "arbitrary")),
        name="prompt_mix",
    )(proj, proj, proj, qlat, qpe, proj, proj, ckvn, ikkpe, tz, w_uv2)


def _page_copy(pt_ref, cache_ref, layer, b, j, buf, slot, sem, feature_major=False):
    src = cache_ref.at[layer, pt_ref[b, j]]
    if feature_major:
        dst = buf.at[slot, :, pl.ds(j * PAGE_SIZE, PAGE_SIZE)]
    else:
        rows = cache_ref.shape[2]
        dst = buf.at[slot, pl.ds(j * rows, rows)]
    return pltpu.make_async_copy(src, dst, sem.at[slot])


def _fetch_pages(step, n_steps, step_coords, copies):
    slot = step % 2

    @pl.when(step == 0)
    def _():
        for c in copies(step_coords(step), slot):
            c.start()

    @pl.when(step + 1 < n_steps)
    def _():
        for c in copies(step_coords(step + 1), 1 - slot):
            c.start()

    for c in copies(step_coords(step), slot):
        c.wait()
    return slot


def _sample_select_kernel(pt_ref, q_ref, w_ref, new_ref, cache_ref, mask_ref, ibuf, s_scr, sem,
                          *, layer, n_pages, n_new, topk, idx_bits, chunk):
    def copies(b, slot):
        return [_page_copy(pt_ref, cache_ref, layer, b, j, ibuf, slot, sem, feature_major=True)
                for j in range(n_pages)]

    slot = _fetch_pages(pl.program_id(0), pl.num_programs(0), lambda s: s, copies)
    past = n_pages * PAGE_SIZE
    n = past + PAGE_SIZE
    qb = q_ref[...].astype(BF16)
    wq = w_ref[...]

    def scores(keys_t):
        s = jnp.maximum(_bdot(qb, keys_t), 0.0) * wq
        acc = s[0:SUBLANES]
        for h in range(1, H_I):
            acc = acc + s[h * SUBLANES:(h + 1) * SUBLANES]
        return acc

    for c0 in range(0, past, chunk):
        s_scr[:, c0:c0 + chunk] = scores(ibuf[slot, :, c0:c0 + chunk])
    s_scr[:, past:] = scores(new_ref[...])
    tok = lax.broadcasted_iota(I32, (SUBLANES, n), 0)
    kpos = lax.broadcasted_iota(I32, (SUBLANES, n), 1)
    valid = (kpos <= past + tok) & (kpos < past + n_new)
    sel = _topk_mask(s_scr[...], valid, kpos, topk, idx_bits) & valid
    mask_ref[...] = sel.astype(F32)


def sample_select(page_table, q_idx, w_idx, cache_idx, layer, ik_new_pad, n_new, topk):
    nb, n_pages = page_table.shape
    past = n_pages * PAGE_SIZE
    n = past + PAGE_SIZE
    kern = functools.partial(_sample_select_kernel, layer=layer, n_pages=n_pages, n_new=n_new,
                             topk=topk, idx_bits=int(math.ceil(math.log2(n))),
                             chunk=min(past, 1024))
    grid_spec = pltpu.PrefetchScalarGridSpec(
        num_scalar_prefetch=1,
        grid=(nb,),
        in_specs=[pl.BlockSpec((None, H_I * SUBLANES, D_I), lambda b, pt: (b, 0, 0)),
                  pl.BlockSpec((None, H_I * SUBLANES, 1), lambda b, pt: (b, 0, 0)),
                  pl.BlockSpec((None, D_I, PAGE_SIZE), lambda b, pt: (b, 0, 0)),
                  pl.BlockSpec(memory_space=pl.ANY)],
        out_specs=pl.BlockSpec((None, SUBLANES, n), lambda b, pt: (b, 0, 0)),
        scratch_shapes=[pltpu.VMEM((2, D_I, past), F32), pltpu.VMEM((SUBLANES, n), F32),
                        pltpu.SemaphoreType.DMA((2,))],
    )
    return pl.pallas_call(
        kern, grid_spec=grid_spec,
        out_shape=jax.ShapeDtypeStruct((nb, SUBLANES, n), F32),
        compiler_params=_cparams(("arbitrary",)),
        name="sample_select",
    )(page_table, q_idx, w_idx, ik_new_pad, cache_idx)


def _sample_dsa_kernel(pt_ref, q_ref, mask_ref, tz_ref, knew_ref, vnew_ref, ck_ref, cv_ref, o_ref,
                       kbuf, vbuf, k_scr, v_scr, sem_k, sem_v, *, layer, n_pages):
    def copies(b, slot):
        out = []
        for j in range(n_pages):
            out.append(_page_copy(pt_ref, ck_ref, layer, b, j, kbuf, slot, sem_k))
            out.append(_page_copy(pt_ref, cv_ref, layer, b, j, vbuf, slot, sem_v))
        return out

    slot = _fetch_pages(pl.program_id(0), pl.num_programs(0), lambda s: s, copies)
    past = n_pages * PAGE_SIZE
    n = past + PAGE_SIZE
    grp = H_A // KV_A
    near = 2 * PAGE_SIZE
    m = mask_ref[...]
    sel = jnp.concatenate([m] * grp, axis=0) > 0.5
    for kv in range(KV_A):
        k_scr[:past, :] = kbuf[slot, pl.ds(kv, past, stride=KV_A), :].astype(BF16)
        v_scr[:past, :] = vbuf[slot, pl.ds(kv, past, stride=KV_A), :].astype(BF16)
        k_scr[past:, :] = knew_ref[:, kv * HD_A:(kv + 1) * HD_A].astype(BF16)
        v_scr[past:, :] = vnew_ref[:, kv * HD_A:(kv + 1) * HD_A].astype(BF16)
        s = _bdot_nt(q_ref[kv], k_scr[...]) * (HD_A ** -0.5)
        bias = jnp.concatenate([jnp.zeros((grp * SUBLANES, n - near), F32), tz_ref[kv]], axis=1)
        p, rl = _softmax_rows(jnp.where(sel, s + bias, NEG_INF))
        o_ref[kv] = _bdot(p, v_scr[...]) * rl


def sample_dsa(page_table, q_a, mask, tz_s, cache_k, cache_v, layer, k_new_pad, v_new_pad):
    nb, n_pages = page_table.shape
    past = n_pages * PAGE_SIZE
    n = past + PAGE_SIZE
    grp = H_A // KV_A
    rows = grp * SUBLANES
    kern = functools.partial(_sample_dsa_kernel, layer=layer, n_pages=n_pages)
    new_spec = pl.BlockSpec((None, PAGE_SIZE, KV_A * HD_A), lambda b, pt: (b, 0, 0))
    grid_spec = pltpu.PrefetchScalarGridSpec(
        num_scalar_prefetch=1,
        grid=(nb,),
        in_specs=[pl.BlockSpec((None, KV_A, rows, HD_A), lambda b, pt: (b, 0, 0, 0)),
                  pl.BlockSpec((None, SUBLANES, n), lambda b, pt: (b, 0, 0)),
                  pl.BlockSpec((KV_A, rows, 2 * PAGE_SIZE), lambda b, pt: (0, 0, 0)),
                  new_spec, new_spec,
                  pl.BlockSpec(memory_space=pl.ANY), pl.BlockSpec(memory_space=pl.ANY)],
        out_specs=pl.BlockSpec((None, KV_A, rows, HD_A), lambda b, pt: (b, 0, 0, 0)),
        scratch_shapes=[pltpu.VMEM((2, past * KV_A, HD_A), F32),
                        pltpu.VMEM((2, past * KV_A, HD_A), F32),
                        pltpu.VMEM((n, HD_A), BF16), pltpu.VMEM((n, HD_A), BF16),
                        pltpu.SemaphoreType.DMA((2,)), pltpu.SemaphoreType.DMA((2,))],
    )
    return pl.pallas_call(
        kern, grid_spec=grid_spec,
        out_shape=jax.ShapeDtypeStruct((nb, KV_A, rows, HD_A), F32),
        compiler_params=_cparams(("arbitrary",)),
        name="sample_dsa",
    )(page_table, q_a, mask, tz_s, k_new_pad, v_new_pad, cache_k, cache_v)


def _sample_mla_kernel(pt_ref, qlat_ref, qpe_ref, cnew_ref, rnew_ref, cc_ref, cr_ref, o_ref,
                       cbuf, rbuf, c_scr, r_scr, sem_c, sem_r, *, layer, n_pages, n_new):
    def copies(b, slot):
        out = []
        for j in range(n_pages):
            out.append(_page_copy(pt_ref, cc_ref, layer, b, j, cbuf, slot, sem_c))
            out.append(_page_copy(pt_ref, cr_ref, layer, b, j, rbuf, slot, sem_r, feature_major=True))
        return out

    slot = _fetch_pages(pl.program_id(0), pl.num_programs(0), lambda s: s, copies)
    past = n_pages * PAGE_SIZE
    n = past + PAGE_SIZE
    c_scr[:past, :] = cbuf[slot].astype(BF16)
    r_scr[:, :past] = rbuf[slot].astype(BF16)
    c_scr[past:, :] = cnew_ref[...].astype(BF16)
    r_scr[:, past:] = rnew_ref[...].astype(BF16)
    rows = H_B * SUBLANES
    s = (_bdot_nt(qlat_ref[...], c_scr[...]) + _bdot(qpe_ref[...], r_scr[...])) * (
        (NOPE_B + ROPE_B) ** -0.5)
    tok = lax.broadcasted_iota(I32, (rows, n), 0) % SUBLANES
    kpos = lax.broadcasted_iota(I32, (rows, n), 1)
    valid = (kpos <= past + tok) & (kpos < past + n_new)
    p, rl = _softmax_rows(jnp.where(valid, s, NEG_INF))
    o_ref[...] = _bdot(p, c_scr[...]) * rl


def sample_mla(page_table, q_lat, q_pe, cache_lat, cache_rope, layer, c_new_pad, r_new_pad, n_new):
    nb, n_pages = page_table.shape
    past = n_pages * PAGE_SIZE
    n = past + PAGE_SIZE
    rows = H_B * SUBLANES
    kern = functools.partial(_sample_mla_kernel, layer=layer, n_pages=n_pages, n_new=n_new)
    grid_spec = pltpu.PrefetchScalarGridSpec(
        num_scalar_prefetch=1,
        grid=(nb,),
        in_specs=[pl.BlockSpec((None, rows, KV_LORA), lambda b, pt: (b, 0, 0)),
                  pl.BlockSpec((None, rows, ROPE_B), lambda b, pt: (b, 0, 0)),
                  pl.BlockSpec((None, PAGE_SIZE, KV_LORA), lambda b, pt: (b, 0, 0)),
                  pl.BlockSpec((None, ROPE_B, PAGE_SIZE), lambda b, pt: (b, 0, 0)),
                  pl.BlockSpec(memory_space=pl.ANY), pl.BlockSpec(memory_space=pl.ANY)],
        out_specs=pl.BlockSpec((None, rows, KV_LORA), lambda b, pt: (b, 0, 0)),
        scratch_shapes=[pltpu.VMEM((2, past, KV_LORA), F32), pltpu.VMEM((2, ROPE_B, past), F32),
                        pltpu.VMEM((n, KV_LORA), BF16), pltpu.VMEM((ROPE_B, n), BF16),
                        pltpu.SemaphoreType.DMA((2,)), pltpu.SemaphoreType.DMA((2,))],
    )
    return pl.pallas_call(
        kern, grid_spec=grid_spec,
        out_shape=jax.ShapeDtypeStruct((nb, rows, KV_LORA), F32),
        compiler_params=_cparams(("arbitrary",)),
        name="sample_mla",
    )(page_table, q_lat, q_pe, c_new_pad, r_new_pad, cache_lat, cache_rope)


def _head_proj_kernel(o_ref, w_ref, y_ref):
    y_ref[...] = _bdot(o_ref[...], w_ref[...])


def head_proj(o_lat, w_uv_h):
    nh, m, c = o_lat.shape
    v = w_uv_h.shape[2]
    return pl.pallas_call(
        _head_proj_kernel,
        grid=(nh,),
        in_specs=[pl.BlockSpec((None, m, c), lambda h: (h, 0, 0)),
                  pl.BlockSpec((None, c, v), lambda h: (h, 0, 0))],
        out_specs=pl.BlockSpec((None, m, v), lambda h: (h, 0, 0)),
        out_shape=jax.ShapeDtypeStruct((nh, m, v), F32),
        compiler_params=_cparams(("arbitrary",)),
        name="head_proj",
    )(o_lat, w_uv_h)


def _rope_half(x, cos, sin):
    half = x.shape[-1] // 2
    x1, x2 = x[:, :half], x[:, half:]
    return jnp.concatenate([x1 * cos - x2 * sin, x1 * sin + x2 * cos], axis=-1)


def _group_norm_gate(o, g, gain):
    mu = jnp.mean(o, axis=-1, keepdims=True)
    d = o - mu
    var = jnp.mean(d * d, axis=-1, keepdims=True)
    on = (d * lax.rsqrt(var + EPS)) * gain
    return (g * jax.nn.sigmoid(g)) * on


def _ret_prompt_kernel(q_ref, k_ref, v_ref, g_ref, cos_ref, sin_ref, dmask_ref, cross_ref,
                       kdec_ref, cdec_ref, gain_ref, og_ref, st_ref, s_scr):
    c = pl.program_id(2)

    @pl.when(c == 0)
    def _():
        s_scr[...] = jnp.zeros_like(s_scr)

    cos, sin = cos_ref[...], sin_ref[...]
    q = _rope_half(q_ref[...], cos, sin)
    k = _rope_half(k_ref[...], cos, sin) * (DK_C ** -0.5)
    v = v_ref[...]
    s_old = s_scr[...]
    a = _bdot_nt(q, k) * dmask_ref[...]
    o = _bdot(a, v) + _bdot(q, s_old) * cross_ref[...]
    s_new = s_old * cdec_ref[...] + _bdot_tn(k * kdec_ref[...], v)
    s_scr[...] = s_new
    og_ref[...] = _group_norm_gate(o, g_ref[...], gain_ref[...]).astype(og_ref.dtype)

    @pl.when(c == pl.num_programs(2) - 1)
    def _():
        st_ref[...] = s_new


def ret_prompt(proj, tabs, gain, nb, seq):
    cos, sin, dmask, cross, kdec, cdec = tabs
    ch = dmask.shape[1]
    nc = seq // ch
    hq = H_C * DK_C
    kern = _ret_prompt_kernel
    per_head = lambda a: pl.BlockSpec((None,) + a.shape[1:], lambda b, h, c: (h,) + (0,) * (a.ndim - 1))
    return pl.pallas_call(
        kern,
        grid=(nb, H_C, nc),
        in_specs=[pl.BlockSpec((ch, DK_C), lambda b, h, c: (b * nc + c, h)),
                  pl.BlockSpec((ch, DK_C), lambda b, h, c: (b * nc + c, hq // DK_C + h)),
                  pl.BlockSpec((ch, DV_C), lambda b, h, c: (b * nc + c, 2 * hq // DV_C + h)),
                  pl.BlockSpec((ch, DV_C), lambda b, h, c: (b * nc + c, 2 * hq // DV_C + H_C + h)),
                  pl.BlockSpec((ch, DK_C // 2), lambda b, h, c: (c, 0)),
                  pl.BlockSpec((ch, DK_C // 2), lambda b, h, c: (c, 0)),
                  per_head(dmask), per_head(cross), per_head(kdec), per_head(cdec),
                  pl.BlockSpec((1, DV_C), lambda b, h, c: (0, h))],
        out_specs=[pl.BlockSpec((ch, DV_C), lambda b, h, c: (b * nc + c, h)),
                   pl.BlockSpec((None, None, DK_C, DV_C), lambda b, h, c: (b, h, 0, 0))],
        out_shape=[jax.ShapeDtypeStruct((nb * seq, H_C * DV_C), BF16),
                   jax.ShapeDtypeStruct((nb, H_C, DK_C, DV_C), F32)],
        scratch_shapes=[pltpu.VMEM((DK_C, DV_C), F32)],
        compiler_params=_cparams(("arbitrary", "arbitrary", "arbitrary")),
        name="ret_prompt",
    )(proj, proj, proj, proj, cos, sin, dmask, cross, kdec, cdec, gain)


def _ret_sample_kernel(p_ref, s0_ref, cos_ref, sin_ref, dmask_ref, cross_ref, kdec_ref, cdec_ref,
                       gain_ref, *refs):
    og_ref, st_ref = refs[-2:]
    cos, sin = cos_ref[...], sin_ref[...]
    hq = H_C * DK_C
    for h in range(H_C):
        q = _rope_half(p_ref[:, h * DK_C:(h + 1) * DK_C], cos, sin)
        k = _rope_half(p_ref[:, hq + h * DK_C:hq + (h + 1) * DK_C], cos, sin) * (DK_C ** -0.5)
        v = p_ref[:, 2 * hq + h * DV_C:2 * hq + (h + 1) * DV_C]
        g = p_ref[:, 2 * hq + (H_C + h) * DV_C:2 * hq + (H_C + h + 1) * DV_C]
        s_old = s0_ref[h]
        a = _bdot_nt(q, k) * dmask_ref[h]
        o = _bdot(a, v) + _bdot(q, s_old) * cross_ref[h]
        st_ref[h] = s_old * cdec_ref[h] + _bdot_tn(k * kdec_ref[h], v)
        og_ref[:, h * DV_C:(h + 1) * DV_C] = _group_norm_gate(
            o, g, gain_ref[:, h * DV_C:(h + 1) * DV_C]).astype(og_ref.dtype)


def ret_sample(proj_pad, state_c, layer, tabs, gain, st_all):
    cos, sin, dmask, cross, kdec, cdec = tabs
    nb = proj_pad.shape[0]
    full = lambda a: pl.BlockSpec(a.shape, lambda b: (0,) * a.ndim)
    ins = [proj_pad, state_c, cos, sin, dmask, cross, kdec, cdec, gain]
    in_specs = [pl.BlockSpec((None, SUBLANES, proj_pad.shape[2]), lambda b: (b, 0, 0)),
                pl.BlockSpec((None, None, H_C, DK_C, DV_C), lambda b: (layer, b, 0, 0, 0)),
                full(cos), full(sin), full(dmask), full(cross), full(kdec), full(cdec), full(gain)]
    aliases = {}
    if st_all is not None:
        ins.append(st_all)
        in_specs.append(pl.BlockSpec(memory_space=pl.ANY))
        aliases = {len(ins) - 1: 1}
    return pl.pallas_call(
        _ret_sample_kernel,
        grid=(nb,),
        in_specs=in_specs,
        out_specs=[pl.BlockSpec((None, SUBLANES, H_C * DV_C), lambda b: (b, 0, 0)),
                   pl.BlockSpec((None, None, H_C, DK_C, DV_C), lambda b: (layer, b, 0, 0, 0))],
        out_shape=[jax.ShapeDtypeStruct((nb, SUBLANES, H_C * DV_C), BF16),
                   jax.ShapeDtypeStruct(state_c.shape, F32)],
        input_output_aliases=aliases,
        compiler_params=_cparams(("arbitrary",)),
        name="ret_sample",
    )(*ins)


ROUTE_LANE0 = N_GROUPS


def _route_kernel(x_ref, g_ref, sh_ref, sc_ref, wr_ref, br_ref, cin_ref,
                  h_ref, mi_ref, mf_ref, cnt_ref, carry_scr):
    i = pl.program_id(0)

    @pl.when(i == 0)
    def _():
        carry_scr[...] = cin_ref[...]

    h = (_rms(x_ref[...]) * g_ref[...]) * (1.0 + sc_ref[...]) + sh_ref[...]
    h_ref[...] = h
    logits = jnp.dot(h, wr_ref[...], preferred_element_type=F32,
                     precision=lax.Precision.HIGHEST) + br_ref[...]
    tm = h.shape[0]
    lane = lax.broadcasted_iota(I32, (tm, LANES), 1)
    big = jnp.int32(LANES)

    def first_max(v):
        m = jnp.max(v, axis=1, keepdims=True)
        return m, jnp.min(jnp.where(v == m, lane, big), axis=1, keepdims=True)

    is_g = lane < N_GROUPS
    mg, gsel = first_max(jnp.where(is_g, logits, NEG_INF))
    gate_g = 1.0 / jnp.sum(jnp.where(is_g, jnp.exp(logits - mg), 0.0), axis=1, keepdims=True)
    lo = ROUTE_LANE0 + gsel * E_PER_GROUP
    le = jnp.where((lane >= lo) & (lane < lo + E_PER_GROUP), logits, NEG_INF)
    m1, i1 = first_max(le)
    m2, i2 = first_max(jnp.where(lane == i1, NEG_INF, le))
    e2 = jnp.exp(m2 - m1)
    w1 = gate_g / (1.0 + e2)
    w2 = gate_g * e2 / (1.0 + e2)

    hit1 = lane == i1
    hit2 = lane == i2
    oh = (hit1 | hit2).astype(BF16)
    r = lax.broadcasted_iota(I32, (tm, tm), 0)
    c = lax.broadcasted_iota(I32, (tm, tm), 1)
    cum = jnp.dot((c < r).astype(BF16), oh, preferred_element_type=F32) + carry_scr[0:1, :]
    rank1 = jnp.sum(jnp.where(hit1, cum, 0.0), axis=1, keepdims=True).astype(I32)
    rank2 = jnp.sum(jnp.where(hit2, cum, 0.0), axis=1, keepdims=True).astype(I32)
    carry_scr[...] = carry_scr[...] + jnp.sum(oh.astype(F32), axis=0, keepdims=True)
    cnt_ref[...] = carry_scr[...]

    mi_ref[...] = jnp.where(lane == 0, i1 - ROUTE_LANE0,
                            jnp.where(lane == 1, i2 - ROUTE_LANE0,
                                      jnp.where(lane == 2, rank1, jnp.where(lane == 3, rank2, 0))))
    mf_ref[...] = jnp.where(lane == 0, w1, jnp.where(lane == 1, w2, 0.0))


def moe_route(x, gain, shift, scale, w_router, b_router, carry_in, rows_per_group, tm=256):
    m, d = x.shape
    tm = min(tm, m)
    steps = max(rows_per_group // tm, 1)
    shift, scale = _per_block(shift, tm), _per_block(scale, tm)
    mod = lambda a: pl.BlockSpec((None,) + a.shape[1:], lambda i: (i // steps, 0, 0))
    return pl.pallas_call(
        _route_kernel,
        grid=(m // tm,),
        in_specs=[pl.BlockSpec((tm, d), lambda i: (i, 0)),
                  pl.BlockSpec((1, d), lambda i: (0, 0)),
                  mod(shift), mod(scale),
                  pl.BlockSpec((d, LANES), lambda i: (0, 0)),
                  pl.BlockSpec((1, LANES), lambda i: (0, 0)),
                  pl.BlockSpec((SUBLANES, LANES), lambda i: (0, 0))],
        out_specs=[pl.BlockSpec((tm, d), lambda i: (i, 0)),
                   pl.BlockSpec((tm, LANES), lambda i: (i, 0)),
                   pl.BlockSpec((tm, LANES), lambda i: (i, 0)),
                   pl.BlockSpec((SUBLANES, LANES), lambda i: (0, 0))],
        out_shape=[jax.ShapeDtypeStruct((m, d), F32),
                   jax.ShapeDtypeStruct((m, LANES), I32),
                   jax.ShapeDtypeStruct((m, LANES), F32),
                   jax.ShapeDtypeStruct((SUBLANES, LANES), F32)],
        scratch_shapes=[pltpu.VMEM((SUBLANES, LANES), F32)],
        compiler_params=_cparams(("arbitrary",)),
        name="moe_route",
    )(x, gain.reshape(1, d), shift, scale, w_router, b_router, carry_in)


def _expert_kernel(te_ref, tok_ref, h_ref, w1_ref, w3_ref, w2_ref, y_ref,
                   xbuf, w1b, w3b, w2b, sem, *, tm):
    i = pl.program_id(0)
    n_used = te_ref[1, 0]
    slot = i % 2

    def row_copy(tile, r, sl):
        return pltpu.make_async_copy(h_ref.at[pl.ds(tok_ref[tile * tm + r], 1)],
                                     xbuf.at[sl, pl.ds(r, 1)], sem.at[sl])

    def start_rows(tile, sl):
        def body(r, carry):
            row_copy(tile, r, sl).start()
            return carry
        lax.fori_loop(0, tm, body, 0)

    @pl.when(i == 0)
    def _():
        start_rows(0, 0)

    @pl.when(i + 1 < n_used)
    def _():
        start_rows(i + 1, 1 - slot)

    @pl.when(i < n_used)
    def _():
        @pl.when(te_ref[2, i] > 0)
        def _():
            w1b[...] = w1_ref[...].astype(BF16)
            w3b[...] = w3_ref[...].astype(BF16)
            w2b[...] = w2_ref[...].astype(BF16)

        def body(r, carry):
            row_copy(i, r, slot).wait()
            return carry
        lax.fori_loop(0, tm, body, 0)

        x = xbuf[slot].astype(BF16)
        a = jnp.dot(x, w1b[...], preferred_element_type=F32)
        b = jnp.dot(x, w3b[...], preferred_element_type=F32)
        act = (a * jax.nn.sigmoid(a)) * b
        y_ref[...] = jnp.dot(act.astype(BF16), w2b[...], preferred_element_type=F32)

    @pl.when(i >= n_used)
    def _():
        y_ref[...] = jnp.zeros_like(y_ref)


def moe_experts(tile_info, tok_of_slot, h, w1, w3, w2, layer, tm):
    n_slots = tok_of_slot.shape[0]
    d = h.shape[1]
    f = w1.shape[3]
    wspec = lambda r, c: pl.BlockSpec((None, None, r, c), lambda i, te, tok: (layer, te[0, i], 0, 0))
    grid_spec = pltpu.PrefetchScalarGridSpec(
        num_scalar_prefetch=2,
        grid=(n_slots // tm,),
        in_specs=[pl.BlockSpec(memory_space=pl.ANY), wspec(d, f), wspec(d, f), wspec(f, d)],
        out_specs=pl.BlockSpec((tm, d), lambda i, te, tok: (i, 0)),
        scratch_shapes=[pltpu.VMEM((2, tm, d), F32), pltpu.VMEM((d, f), BF16),
                        pltpu.VMEM((d, f), BF16), pltpu.VMEM((f, d), BF16),
                        pltpu.SemaphoreType.DMA((2,))],
    )
    return pl.pallas_call(
        functools.partial(_expert_kernel, tm=tm), grid_spec=grid_spec,
        out_shape=jax.ShapeDtypeStruct((n_slots, d), F32),
        compiler_params=_cparams(("arbitrary",)),
        name="moe_experts",
    )(tile_info, tok_of_slot, h, w1, w3, w2)


def _combine_kernel(pos_ref, x_ref, gate_ref, mf_ref, ys_ref, o_ref, buf, sem, *, tb, tok0):
    base = tok0 + pl.program_id(0) * tb

    def issue(t, carry):
        for j in range(2):
            pltpu.make_async_copy(ys_ref.at[pl.ds(pos_ref[2 * (base + t) + j], 1)],
                                  buf.at[j, pl.ds(t, 1)], sem).start()
        return carry

    lax.fori_loop(0, tb, issue, 0)

    def drain(t, carry):
        for j in range(2):
            pltpu.make_async_copy(ys_ref.at[pl.ds(0, 1)], buf.at[j, pl.ds(0, 1)], sem).wait()
        return carry

    lax.fori_loop(0, tb, drain, 0)
    y = mf_ref[:, 0:1] * buf[0] + mf_ref[:, 1:2] * buf[1]
    o_ref[...] = x_ref[...] + gate_ref[...] * y


def moe_combine(pos_flat, x, gate, mf, ys, tok0, rows_per_group, tb=256):
    m, d = x.shape
    tb = min(tb, m)
    steps = max(rows_per_group // tb, 1)
    gate = _per_block(gate, tb)
    grid_spec = pltpu.PrefetchScalarGridSpec(
        num_scalar_prefetch=1,
        grid=(m // tb,),
        in_specs=[pl.BlockSpec((tb, d), lambda i, p: (i, 0)),
                  pl.BlockSpec((None,) + gate.shape[1:], lambda i, p: (i // steps, 0, 0)),
                  pl.BlockSpec((tb, LANES), lambda i, p: (i, 0)),
                  pl.BlockSpec(memory_space=pl.ANY)],
        out_specs=pl.BlockSpec((tb, d), lambda i, p: (i, 0)),
        scratch_shapes=[pltpu.VMEM((2, tb, d), F32), pltpu.SemaphoreType.DMA(())],
    )
    return pl.pallas_call(
        functools.partial(_combine_kernel, tb=tb, tok0=tok0), grid_spec=grid_spec,
        out_shape=jax.ShapeDtypeStruct((m, d), F32),
        compiler_params=_cparams(("arbitrary",)),
        name="moe_combine",
    )(pos_flat, x, gate, mf, ys)


def _final_norm_kernel(x_ref, g_ref, o_ref):
    o_ref[...] = _rms(x_ref[...]) * g_ref[...]


def final_norm(x, gain, tm=512):
    m, d = x.shape
    tm = min(tm, m)
    return pl.pallas_call(
        _final_norm_kernel,
        grid=(m // tm,),
        in_specs=[pl.BlockSpec((tm, d), lambda i: (i, 0)), pl.BlockSpec((1, d), lambda i: (0, 0))],
        out_specs=pl.BlockSpec((tm, d), lambda i: (i, 0)),
        out_shape=jax.ShapeDtypeStruct((m, d), F32),
        compiler_params=_cparams(("arbitrary",)),
        name="final_norm",
    )(x, gain.reshape(1, d))


def _t5_bucket(dist):
    max_exact = N_BUCKETS // 2
    n = jnp.maximum(dist, 0)
    large = max_exact + (jnp.log(jnp.maximum(n, 1).astype(F32) / max_exact)
                         / math.log(MAX_DISTANCE / max_exact) * (N_BUCKETS - max_exact)).astype(I32)
    large = jnp.minimum(large, N_BUCKETS - 1)
    return jnp.where(n < max_exact, n, large)


def _rope_cos_sin(pos, half):
    inv = ROPE_BASE ** (-jnp.arange(half, dtype=F32) / half)
    ang = pos.astype(F32)[:, None] * inv[None, :]
    return jnp.cos(ang), jnp.sin(ang)


def _even_rope_tables(pos):
    cos, sin = _rope_cos_sin(pos, ROPE_B // 2)
    c64 = jnp.concatenate([cos, cos], axis=1)
    s64 = jnp.concatenate([-sin, sin], axis=1)
    one = jnp.ones_like(c64)
    zero = jnp.zeros_like(s64)
    return (jnp.tile(c64, (1, H_B)), jnp.tile(s64, (1, H_B)),
            jnp.concatenate([one, c64], axis=1), jnp.concatenate([zero, s64], axis=1))


def _retention_tables(pos, c):
    log_g = jnp.log1p(-(2.0 ** (-5.0 - jnp.arange(H_C, dtype=F32))))
    i = jnp.arange(c, dtype=F32)
    diff = i[:, None] - i[None, :]
    dmask = jnp.where(diff >= 0, jnp.exp(jnp.maximum(diff, 0.0)[None] * log_g[:, None, None]), 0.0)
    cross = jnp.exp((i[:, None] + 1.0) * log_g[None, :])
    kdec = jnp.exp((c - 1.0 - i)[:, None] * log_g[None, :])
    chunk_dec = jnp.exp(c * log_g)
    cos, sin = _rope_cos_sin(pos, DK_C // 2)
    return dmask, cross.T[:, :, None], kdec.T[:, :, None], chunk_dec[:, None, None], cos, sin


def _pad_rows(a, axis, n):
    pad = [(0, 0)] * a.ndim
    pad[axis] = (0, n - a.shape[axis])
    return jnp.pad(a, pad)


def _even_weight_layout(w_in):
    qa, ka, va, iq, ik, iw, cq, ckv, kpe = jnp.split(
        w_in, np.cumsum([1024, 256, 256, 1024, 64, 16, 512, 256, 64])[:-1].tolist(), axis=1)
    pad = jnp.zeros((w_in.shape[0], EV_COLS - EV_IW - H_I), w_in.dtype)
    return jnp.concatenate([qa, iq, cq, ka, va, ckv, ik, kpe, iw, pad], axis=1)


def _moe_block(xp, xs, mods_p, mods_s, gain, w_router, b_router, w1, w3, w2, layer, seq, tm_e=256):
    sh_p, sc_p, gt_p = mods_p
    sh_s, sc_s, gt_s = mods_s
    n_p, n_s = xp.shape[0], xs.shape[0]
    zero_carry = jnp.zeros((SUBLANES, LANES), F32)
    h_p, mi_p, mf_p, cnt_p = moe_route(xp, gain, sh_p, sc_p, w_router, b_router, zero_carry, seq)
    h_s, mi_s, mf_s, cnt = moe_route(xs, gain, sh_s, sc_s, w_router, b_router, cnt_p, 1)
    h = jnp.concatenate([h_p, h_s], axis=0)
    mi = jnp.concatenate([mi_p, mi_s], axis=0)
    eid, rank = mi[:, 0:2], mi[:, 2:4]
    counts = cnt[0, ROUTE_LANE0:ROUTE_LANE0 + N_EXPERTS].astype(I32)
    padded = ((counts + tm_e - 1) // tm_e) * tm_e
    ends = jnp.cumsum(padded)
    off = ends - padded
    pos = (off[eid] + rank).reshape(-1).astype(I32)
    n_tiles = (2 * (n_p + n_s)) // tm_e + N_EXPERTS
    starts = jnp.arange(n_tiles, dtype=I32) * tm_e
    tile_e = jnp.searchsorted(ends, starts, side="right").astype(I32)
    valid = (starts < ends[-1]).astype(I32)
    last_e = jnp.max(jnp.where(counts > 0, jnp.arange(N_EXPERTS, dtype=I32), 0))
    tile_e = jnp.where(valid > 0, jnp.minimum(tile_e, N_EXPERTS - 1), last_e)
    changed = jnp.concatenate([jnp.ones((1,), I32), (tile_e[1:] != tile_e[:-1]).astype(I32)])
    tile_info = jnp.stack([tile_e, jnp.broadcast_to(jnp.sum(valid), (n_tiles,)), changed])
    tok_of_slot = jnp.zeros((n_tiles * tm_e,), I32).at[pos].set(
        jnp.arange(pos.shape[0], dtype=I32) // 2)
    ys = moe_experts(tile_info, tok_of_slot, h, w1, w3, w2, layer, tm_e)
    xp = moe_combine(pos, xp, gt_p, mf_p, ys, 0, seq)
    xs = moe_combine(pos, xs, gt_s, mf_s, ys, n_p, 1)
    return xp, xs


def kernel(x_prompt, x_sample, cache_a_k, cache_a_v, cache_a_idx, cache_b_latent, cache_b_rope, state_c,
           page_table, c_prompt, c_sample, rel_bias, w_ada, b_ada, norm_mix, norm_ffn, norm_final,
           w_in_even, q_norm_b, kv_norm_b, w_uq_b, w_uk_b, w_uv_b, w_out_even, w_in_odd, gn_gain_c,
           w_out_odd, w_rg, b_rg, w_re, b_re, w1, w3, w2):
    nb, seq, d = x_prompt.shape
    db, ds, _ = x_sample.shape
    n_p, n_s = nb * seq, db * ds
    past = page_table.shape[1] * PAGE_SIZE
    topk_p = min(TOPK_MAX, seq // 4)
    topk_s = min(TOPK_MAX, (past + ds) // 4)
    pos_p = jnp.arange(seq, dtype=I32)
    pos_s = past + jnp.arange(ds, dtype=I32)

    n_seq = nb + db
    r_pad = -(-n_seq // SUBLANES) * SUBLANES
    c_all = _pad_rows(jnp.concatenate([c_prompt, c_sample], axis=0), 0, r_pad)
    ada = ada_all(c_all, w_ada, b_ada)

    def mods(l):
        mp = ada[l, :nb].reshape(nb, 6, 1, d)
        ms = ada[l, nb:n_seq].reshape(db, 6, d)
        ms = jnp.repeat(ms[:, :, None, :], ds, axis=2)
        ms = jnp.moveaxis(ms, 1, 0).reshape(6, 1, n_s, d)
        return [mp[:, j] for j in range(6)], [ms[j] for j in range(6)]

    rb = rel_bias - rel_bias[N_BUCKETS - 1][None, :]
    ii = jnp.arange(QBLK, dtype=I32)[:, None]
    cc = jnp.arange(2 * QBLK, dtype=I32)[None, :]
    tz_p = jnp.moveaxis(rb[_t5_bucket(ii - cc + QBLK)], -1, 0)
    tt = jnp.minimum(jnp.arange(SUBLANES, dtype=I32), ds - 1)[:, None]
    near0 = past + PAGE_SIZE - 2 * PAGE_SIZE
    tz_s = jnp.moveaxis(rb[_t5_bucket(past + tt - (near0 + cc))], -1, 0)
    grp = H_A // KV_A
    tz_s = tz_s.reshape(KV_A, grp * SUBLANES, 2 * PAGE_SIZE)

    ev_tabs_p = _even_rope_tables(pos_p)
    ev_tabs_s = tuple(jnp.tile(t, (db, 1)) for t in _even_rope_tables(pos_s))
    ch_p = min(RET_CHUNK, seq)
    dmask_p, cross_p, kdec_p, cdec_p, cos_p, sin_p = _retention_tables(pos_p, ch_p)
    ret_tabs_p = (cos_p, sin_p, dmask_p, cross_p, kdec_p, cdec_p)
    dmask_s, cross_s, kdec_s, cdec_s, cos_s, sin_s = _retention_tables(pos_s, ds)
    ret_tabs_s = (_pad_rows(cos_s, 0, SUBLANES), _pad_rows(sin_s, 0, SUBLANES),
                  _pad_rows(_pad_rows(dmask_s, 1, SUBLANES), 2, SUBLANES),
                  _pad_rows(cross_s, 1, SUBLANES), _pad_rows(kdec_s, 1, SUBLANES), cdec_s)

    ck = cache_a_k.reshape(cache_a_k.shape[:2] + (PAGE_SIZE * KV_A, HD_A))
    cv = cache_a_v.reshape(cache_a_v.shape[:2] + (PAGE_SIZE * KV_A, HD_A))
    cidx_t = jnp.swapaxes(cache_a_idx, 2, 3)
    crope_t = jnp.swapaxes(cache_b_rope, 2, 3)

    xp = x_prompt.reshape(n_p, d)
    xs = x_sample.reshape(n_s, d)
    outs_p = {k: [] for k in ("k", "v", "idx", "lat", "rope", "st")}
    outs_s = {k: [] for k in ("k", "v", "idx", "lat", "rope")}
    st_s_all = None

    def head_tok_rows(a, heads, width):
        a = a.reshape(db, ds, heads, width).transpose(0, 2, 1, 3)
        return _pad_rows(a, 2, SUBLANES).reshape(db, heads * SUBLANES, width)

    def new_page(a):
        return _pad_rows(a.reshape(db, ds, a.shape[-1]), 1, PAGE_SIZE)

    for l in range(DEPTH):
        mp, ms = mods(l)
        if l % 2 == 0:
            e = l // 2
            w_in = _even_weight_layout(w_in_even[e])
            qn, kvn = q_norm_b[e].reshape(1, -1), kv_norm_b[e].reshape(1, -1)
            wuq = w_uq_b[e].reshape(Q_LORA, H_B, NOPE_B + ROPE_B)
            wuq_r = jnp.concatenate([wuq[:, :, :NOPE_B].reshape(Q_LORA, -1),
                                     wuq[:, :, NOPE_B:].reshape(Q_LORA, -1)], axis=1)
            wuk_t = jnp.transpose(w_uk_b[e], (1, 2, 0))
            wuv2 = w_uv_b[e].reshape(KV_LORA, H_B * V_B)
            wuv_h = jnp.transpose(w_uv_b[e], (1, 0, 2))

            proj = nm_matmul(xp, norm_mix[l], mp[0], mp[1], w_in, seq)
            qlat, qpe, ckvn, ikkpe = even_post(proj, qn, kvn, wuq_r, wuk_t, ev_tabs_p)
            mix = prompt_mix(proj, qlat, qpe, ckvn, ikkpe, tz_p, wuv2, nb, seq, topk_p)
            xp = mm_residual(mix, w_out_even[e], xp, mp[2], seq)
            outs_p["k"].append(proj[:, EV_KA:EV_KA + 256].reshape(nb, seq, KV_A, HD_A))
            outs_p["v"].append(proj[:, EV_VA:EV_VA + 256].reshape(nb, seq, KV_A, HD_A))
            outs_p["idx"].append(ikkpe[:, :D_I].reshape(nb, seq, D_I))
            outs_p["lat"].append(ckvn.reshape(nb, seq, KV_LORA))
            outs_p["rope"].append(ikkpe[:, D_I:].reshape(nb, seq, ROPE_B))

            proj_s = nm_matmul(xs, norm_mix[l], ms[0], ms[1], w_in, 1)
            qlat_s, qpe_s, ckvn_s, ikkpe_s = even_post(proj_s, qn, kvn, wuq_r, wuk_t, ev_tabs_s)
            ka_s = proj_s[:, EV_KA:EV_KA + 256]
            va_s = proj_s[:, EV_VA:EV_VA + 256]
            ik_s, kpe_s = ikkpe_s[:, :D_I], ikkpe_s[:, D_I:]
            q_idx = head_tok_rows(proj_s[:, EV_IQ:EV_IQ + H_I * D_I], H_I, D_I)
            w_idx = head_tok_rows(proj_s[:, EV_IW:EV_IW + H_I] * (H_I ** -0.5 * D_I ** -0.5), H_I, 1)
            mask = sample_select(page_table, q_idx, w_idx, cidx_t, e,
                                 jnp.swapaxes(new_page(ik_s), 1, 2), ds, topk_s)
            q_a = head_tok_rows(proj_s[:, EV_QA:EV_QA + H_A * HD_A], H_A, HD_A)
            q_a = q_a.reshape(db, KV_A, grp * SUBLANES, HD_A)
            oa = sample_dsa(page_table, q_a, mask, tz_s, ck, cv, e, new_page(ka_s), new_page(va_s))
            oa = oa.reshape(db, H_A, SUBLANES, HD_A)[:, :, :ds].transpose(0, 2, 1, 3).reshape(n_s, -1)
            o_lat = sample_mla(page_table, head_tok_rows(qlat_s, H_B, KV_LORA),
                               head_tok_rows(qpe_s, H_B, ROPE_B), cache_b_latent, crope_t, e,
                               new_page(ckvn_s), jnp.swapaxes(new_page(kpe_s), 1, 2), ds)
            o_lat = o_lat.reshape(db, H_B, SUBLANES, KV_LORA)[:, :, :ds]
            o_lat = o_lat.transpose(1, 0, 2, 3).reshape(H_B, n_s, KV_LORA)
            ob = head_proj(o_lat, wuv_h).transpose(1, 0, 2).reshape(n_s, -1)
            mix_s = jnp.concatenate([oa, ob], axis=1)
            xs = mm_residual(mix_s, w_out_even[e], xs, ms[2], 1)
            outs_s["k"].append(ka_s.reshape(db, ds, KV_A, HD_A))
            outs_s["v"].append(va_s.reshape(db, ds, KV_A, HD_A))
            outs_s["idx"].append(ik_s.reshape(db, ds, D_I))
            outs_s["lat"].append(ckvn_s.reshape(db, ds, KV_LORA))
            outs_s["rope"].append(kpe_s.reshape(db, ds, ROPE_B))
        else:
            o = l // 2
            gain = gn_gain_c[o].reshape(1, -1)
            proj = nm_matmul(xp, norm_mix[l], mp[0], mp[1], w_in_odd[o], seq)
            og, st_p = ret_prompt(proj, ret_tabs_p, gain, nb, seq)
            xp = mm_residual(og, w_out_odd[o], xp, mp[2], seq)
            outs_p["st"].append(st_p)
            proj_s = nm_matmul(xs, norm_mix[l], ms[0], ms[1], w_in_odd[o], 1)
            proj_s = _pad_rows(proj_s.reshape(db, ds, -1), 1, SUBLANES)
            og_s, st_s_all = ret_sample(proj_s, state_c, o, ret_tabs_s, gain, st_s_all)
            og_s = og_s[:, :ds].reshape(n_s, -1)
            xs = mm_residual(og_s, w_out_odd[o], xs, ms[2], 1)

        w_router = jnp.concatenate(
            [w_rg[l], w_re[l], jnp.zeros((d, LANES - N_GROUPS - N_EXPERTS), F32)], axis=1)
        b_router = jnp.concatenate(
            [b_rg[l], b_re[l], jnp.zeros((LANES - N_GROUPS - N_EXPERTS,), F32)]).reshape(1, LANES)
        xp, xs = _moe_block(xp, xs, (mp[3], mp[4], mp[5]), (ms[3], ms[4], ms[5]), norm_ffn[l],
                            w_router, b_router, w1, w3, w2, l, seq)

    y_p = final_norm(xp, norm_final).reshape(nb, seq, d)
    y_s = final_norm(xs, norm_final).reshape(db, ds, d)
    return (y_p, y_s,
            jnp.stack(outs_p["k"]), jnp.stack(outs_p["v"]), jnp.stack(outs_p["idx"]),
            jnp.stack(outs_p["lat"]), jnp.stack(outs_p["rope"]), jnp.stack(outs_p["st"]),
            jnp.stack(outs_s["k"]), jnp.stack(outs_s["v"]), jnp.stack(outs_s["idx"]),
            jnp.stack(outs_s["lat"]), jnp.stack(outs_s["rope"]), st_s_all)
```

```python
import functools
import math

import numpy as np
import jax
import jax.numpy as jnp
from jax import lax
from jax.experimental import pallas as pl
from jax.experimental.pallas import tpu as pltpu

F32 = jnp.float32
BF16 = jnp.bfloat16
I32 = jnp.int32

D_MODEL = 2048
DEPTH = 4
PAST_LEN = 8192
PAGE_SIZE = 128
H_A, HD_A, KV_A = 8, 128, 2
H_I, D_I = 16, 64
TOPK_MAX = 256
N_BUCKETS, MAX_DISTANCE = 32, 128
H_B, Q_LORA, KV_LORA, NOPE_B, ROPE_B, V_B = 8, 512, 256, 128, 64, 128
ROPE_BASE = 10000.0
H_C = 8
DK_C = D_MODEL // H_C
DV_C = 2 * D_MODEL // H_C
RET_CHUNK = 128
N_GROUPS, E_PER_GROUP = 4, 8
N_EXPERTS = N_GROUPS * E_PER_GROUP
D_EXPERT = 512
EPS = 1e-6

LANES = 128
SUBLANES = 8
VMEM_LIMIT = 52 * 1024 * 1024

EV_QA, EV_IQ, EV_CQ, EV_KA, EV_VA, EV_CKV, EV_IK, EV_KPE, EV_IW = (
    0, 1024, 2048, 2560, 2816, 3072, 3328, 3392, 3456)
EV_COLS = 3584
QBLK = 128
NEG_INF = float("-inf")


def _cparams(sem, vmem=VMEM_LIMIT):
    return pltpu.CompilerParams(dimension_semantics=sem, vmem_limit_bytes=vmem)


def _bdot(a, b):
    return jnp.dot(a.astype(BF16), b.astype(BF16), preferred_element_type=F32)


def _bdot_nt(a, b):
    return lax.dot_general(a.astype(BF16), b.astype(BF16), (((1,), (1,)), ((), ())),
                           preferred_element_type=F32)


def _bdot_tn(a, b):
    return lax.dot_general(a.astype(BF16), b.astype(BF16), (((0,), (0,)), ((), ())),
                           preferred_element_type=F32)


def _rms(x):
    return x * lax.rsqrt(jnp.mean(x * x, axis=-1, keepdims=True) + EPS)


def _ada_kernel(c_ref, w_ref, b_ref, o_ref):
    c = c_ref[...]
    a = c * jax.nn.sigmoid(c)
    o_ref[...] = _bdot(a, w_ref[...]) + b_ref[...]


def ada_all(c, w_ada, b_ada):
    r, d = c.shape
    nl, _, n = w_ada.shape
    tn = 1024
    return pl.pallas_call(
        _ada_kernel,
        grid=(nl, n // tn),
        in_specs=[pl.BlockSpec((r, d), lambda l, j: (0, 0)),
                  pl.BlockSpec((None, d, tn), lambda l, j: (l, 0, j)),
                  pl.BlockSpec((None, 1, tn), lambda l, j: (l, 0, j))],
        out_specs=pl.BlockSpec((None, r, tn), lambda l, j: (l, 0, j)),
        out_shape=jax.ShapeDtypeStruct((nl, r, n), F32),
        compiler_params=_cparams(("arbitrary", "arbitrary")),
        name="ada_all",
    )(c, w_ada, b_ada.reshape(nl, 1, n))


def _nm_matmul_kernel(x_ref, g_ref, sh_ref, sc_ref, w_ref, o_ref, h_scr):
    @pl.when(pl.program_id(1) == 0)
    def _():
        rows = min(256, x_ref.shape[0])
        per_token = sh_ref.shape[0] != 1

        def body(r, carry):
            sl = pl.ds(pl.multiple_of(r * rows, rows), rows)
            sc = sc_ref[sl, :] if per_token else sc_ref[...]
            sh = sh_ref[sl, :] if per_token else sh_ref[...]
            h = (_rms(x_ref[sl, :]) * g_ref[...]) * (1.0 + sc) + sh
            h_scr[sl, :] = h.astype(BF16)
            return carry

        lax.fori_loop(0, x_ref.shape[0] // rows, body, 0)

    o_ref[...] = jnp.dot(h_scr[...], w_ref[...].astype(BF16), preferred_element_type=F32)


def _per_block(mod, tm):
    if mod.shape[1] == 1:
        return mod
    return mod.reshape(-1, tm, mod.shape[-1])


def _mod_spec(mod, tm, rows_per_group):
    steps = max(rows_per_group // tm, 1)
    return pl.BlockSpec((None,) + mod.shape[1:], lambda i, j: (i // steps, 0, 0))


def nm_matmul(x, gain, shift, scale, w, rows_per_group, tm=2048, tn=512):
    m, d = x.shape
    n = w.shape[1]
    tm = min(tm, m)
    shift, scale = _per_block(shift, tm), _per_block(scale, tm)
    return pl.pallas_call(
        _nm_matmul_kernel,
        grid=(m // tm, n // tn),
        in_specs=[pl.BlockSpec((tm, d), lambda i, j: (i, 0), pipeline_mode=pl.Buffered(1)),
                  pl.BlockSpec((1, d), lambda i, j: (0, 0)),
                  _mod_spec(shift, tm, rows_per_group),
                  _mod_spec(scale, tm, rows_per_group),
                  pl.BlockSpec((d, tn), lambda i, j: (0, j))],
        out_specs=pl.BlockSpec((tm, tn), lambda i, j: (i, j)),
        out_shape=jax.ShapeDtypeStruct((m, n), F32),
        scratch_shapes=[pltpu.VMEM((tm, d), BF16)],
        compiler_params=_cparams(("arbitrary", "arbitrary")),
        name="nm_matmul",
    )(x, gain.reshape(1, d), shift, scale, w)


def _mm_res_kernel(a_ref, w_ref, x_ref, gate_ref, o_ref):
    o_ref[...] = x_ref[...] + gate_ref[...] * _bdot(a_ref[...], w_ref[...])


def mm_residual(a, w, x, gate, rows_per_group, tm=1024, tn=512):
    m, k = a.shape
    n = w.shape[1]
    tm = min(tm, m)
    steps = max(rows_per_group // tm, 1)
    gate = _per_block(gate, tm)
    return pl.pallas_call(
        _mm_res_kernel,
        grid=(m // tm, n // tn),
        in_specs=[pl.BlockSpec((tm, k), lambda i, j: (i, 0)),
                  pl.BlockSpec((k, tn), lambda i, j: (0, j)),
                  pl.BlockSpec((tm, tn), lambda i, j: (i, j)),
                  pl.BlockSpec((None, gate.shape[1], tn), lambda i, j: (i // steps, 0, j))],
        out_specs=pl.BlockSpec((tm, tn), lambda i, j: (i, j)),
        out_shape=jax.ShapeDtypeStruct((m, n), F32),
        compiler_params=_cparams(("arbitrary", "arbitrary")),
        name="mm_residual",
    )(a, w, x, gate)


def _swap_halves(x, half):
    n = x.shape[-1]
    lane = lax.broadcasted_iota(I32, x.shape, x.ndim - 1)
    return jnp.where(lane % (2 * half) < half,
                     pltpu.roll(x, n - half, x.ndim - 1), pltpu.roll(x, half, x.ndim - 1))


def _even_post_kernel(cq_ref, ckv_ref, slab_ref, iq_ref, qn_ref, kvn_ref, wuq_ref, wuk_ref,
                      cq_c_ref, cq_s_ref, cs_c_ref, cs_s_ref,
                      qlat_ref, qpe_ref, ckvn_ref, ikkpe_ref, iqh_ref):
    cqn = _rms(cq_ref[...]) * qn_ref[...]
    qb = _bdot(cqn, wuq_ref[...])
    nope = H_B * NOPE_B
    for h in range(H_B):
        qlat_ref[h] = _bdot(qb[:, h * NOPE_B:(h + 1) * NOPE_B], wuk_ref[h])
    qr = qb[:, nope:]
    qpe = qr * cq_c_ref[...] + _swap_halves(qr, ROPE_B // 2) * cq_s_ref[...]
    for h in range(H_B):
        qpe_ref[h] = qpe[:, h * ROPE_B:(h + 1) * ROPE_B]
    for h in range(H_I):
        iqh_ref[h] = iq_ref[:, h * D_I:(h + 1) * D_I]
    ckvn_ref[...] = _rms(ckv_ref[...]) * kvn_ref[...]
    slab = slab_ref[:, :LANES]
    ikkpe_ref[...] = slab * cs_c_ref[...] + _swap_halves(slab, ROPE_B // 2) * cs_s_ref[...]


def even_post(proj, q_norm, kv_norm, w_uq_r, w_uk_t, tabs, tm=256):
    m = proj.shape[0]
    tm = min(tm, m)
    cq_c, cq_s, cs_c, cs_s = tabs
    nt = cq_c.shape[0] // tm
    tab = lambda w: pl.BlockSpec((tm, w), lambda i: (i % nt, 0))
    full = lambda a: pl.BlockSpec(a.shape, lambda i: (0,) * a.ndim)
    return pl.pallas_call(
        _even_post_kernel,
        grid=(m // tm,),
        in_specs=[pl.BlockSpec((tm, Q_LORA), lambda i: (i, EV_CQ // Q_LORA)),
                  pl.BlockSpec((tm, KV_LORA), lambda i: (i, EV_CKV // KV_LORA)),
                  pl.BlockSpec((tm, 256), lambda i: (i, EV_IK // 256)),
                  pl.BlockSpec((tm, H_I * D_I), lambda i: (i, EV_IQ // (H_I * D_I))),
                  full(q_norm), full(kv_norm), full(w_uq_r), full(w_uk_t),
                  tab(512), tab(512), tab(LANES), tab(LANES)],
        out_specs=[pl.BlockSpec((H_B, tm, KV_LORA), lambda i: (0, i, 0)),
                   pl.BlockSpec((H_B, tm, ROPE_B), lambda i: (0, i, 0)),
                   pl.BlockSpec((tm, KV_LORA), lambda i: (i, 0)),
                   pl.BlockSpec((tm, LANES), lambda i: (i, 0)),
                   pl.BlockSpec((H_I, tm, D_I), lambda i: (0, i, 0))],
        out_shape=[jax.ShapeDtypeStruct((H_B, m, KV_LORA), F32),
                   jax.ShapeDtypeStruct((H_B, m, ROPE_B), F32),
                   jax.ShapeDtypeStruct((m, KV_LORA), F32),
                   jax.ShapeDtypeStruct((m, LANES), F32),
                   jax.ShapeDtypeStruct((H_I, m, D_I), F32)],
        compiler_params=_cparams(("arbitrary",)),
        name="even_post",
    )(proj, proj, proj, proj, q_norm, kv_norm, w_uq_r, w_uk_t, cq_c, cq_s, cs_c, cs_s)


def _topk_mask(score, valid, kpos, topk, idx_bits):
    sm = jnp.where(valid, jnp.where(score == 0.0, 0.0, score), NEG_INF)
    bits = pltpu.bitcast(sm, I32)
    key = jnp.where(bits < 0, bits ^ jnp.int32(0x7FFFFFFF), bits)

    def count(m):
        return jnp.sum(m.astype(I32), axis=1, keepdims=True)

    int_min = jnp.int32(-2 ** 31)
    t0 = jnp.where(count(key >= 0) >= topk, jnp.int32(0), int_min)

    def bit_step(i, t):
        cand = t | (jnp.int32(1) << (30 - i))
        return jnp.where(count(key >= cand) >= topk, cand, t)

    thr = lax.fori_loop(0, 31, bit_step, jnp.broadcast_to(t0, (score.shape[0], 1)))
    above = key > thr
    ties = key == thr
    need = topk - count(above)
    neg_inf_key = jnp.int32(-0x7F800001)
    excess = jnp.max(jnp.where(thr > neg_inf_key, count(ties) - need, 0))

    def lowest_index_ties(_):
        def idx_step(i, j):
            cand = j | (jnp.int32(1) << (idx_bits - 1 - i))
            return jnp.where(count(ties & (kpos < cand)) < need, cand, j)
        return lax.fori_loop(0, idx_bits, idx_step, jnp.zeros((score.shape[0], 1), I32))

    def all_ties(_):
        return jnp.full((score.shape[0], 1), score.shape[1], I32)

    last = lax.cond(excess > 0, lowest_index_ties, all_ties, 0)
    return above | (ties & (kpos <= last))


def _softmax_rows(s):
    p = jnp.exp(s - jnp.max(s, axis=1, keepdims=True))
    return p, 1.0 / jnp.sum(p, axis=1, keepdims=True)


HEAD_STACK = 4


def _prompt_mix_kernel(qa_ref, iq_ref, qslab_ref, qlat_ref, qpe_ref, ka_ref, va_ref, ckv_ref,
                       ikkpe_ref, tz_ref, wuv_ref, o_ref, s_scr, *, topk, idx_bits):
    qi = pl.program_id(1)
    tq = qa_ref.shape[0]
    nk = ka_ref.shape[0]
    hs = HEAD_STACK
    q0 = pl.multiple_of(qi * tq, tq)
    qpos = q0 + lax.broadcasted_iota(I32, (tq, nk), 0)
    kpos = lax.broadcasted_iota(I32, (tq, nk), 1)
    causal = kpos <= qpos

    ikb = ikkpe_ref[:, :D_I].astype(BF16)
    kpeb = ikkpe_ref[:, D_I:].astype(BF16)
    wq = qslab_ref[:, LANES:LANES + H_I] * (H_I ** -0.5 * D_I ** -0.5)
    score = jnp.zeros((tq, nk), F32)
    for g in range(H_I // hs):
        s = _bdot_nt(iq_ref[g * hs:(g + 1) * hs].reshape(hs * tq, D_I), ikb)
        for j in range(hs):
            h = g * hs + j
            score = score + wq[:, h:h + 1] * jnp.maximum(s[j * tq:(j + 1) * tq], 0.0)
    sel = _topk_mask(score, causal, kpos, topk, idx_bits) & causal

    grp = H_A // KV_A
    for kv in range(KV_A):
        kk = ka_ref[:, kv * HD_A:(kv + 1) * HD_A].astype(BF16)
        vv = va_ref[:, kv * HD_A:(kv + 1) * HD_A].astype(BF16)
        qs = jnp.concatenate([qa_ref[:, (kv * grp + g) * HD_A:(kv * grp + g + 1) * HD_A]
                              for g in range(grp)], axis=0)
        s_scr[...] = _bdot_nt(qs, kk) * (HD_A ** -0.5)
        for g in range(grp):
            h = kv * grp + g
            s_scr[g * tq:(g + 1) * tq, pl.ds(q0, tq)] += tz_ref[h, :, tq:]

            @pl.when(qi > 0)
            def _():
                s_scr[g * tq:(g + 1) * tq, pl.ds(q0 - tq, tq)] += tz_ref[h, :, :tq]

        s = jnp.where(sel[None], s_scr[...].reshape(grp, tq, nk), NEG_INF).reshape(grp * tq, nk)
        p, rl = _softmax_rows(s)
        o = _bdot(p, vv) * rl
        for g in range(grp):
            h = kv * grp + g
            o_ref[:, h * HD_A:(h + 1) * HD_A] = o[g * tq:(g + 1) * tq].astype(o_ref.dtype)

    ckvb = ckv_ref[...].astype(BF16)
    off = H_A * HD_A
    for g in range(H_B // hs):
        ql = qlat_ref[g * hs:(g + 1) * hs].reshape(hs * tq, KV_LORA)
        qp = qpe_ref[g * hs:(g + 1) * hs].reshape(hs * tq, ROPE_B)
        s = (_bdot_nt(ql, ckvb) + _bdot_nt(qp, kpeb)) * ((NOPE_B + ROPE_B) ** -0.5)
        s = jnp.where(causal[None], s.reshape(hs, tq, nk), NEG_INF).reshape(hs * tq, nk)
        p, rl = _softmax_rows(s)
        o_lat = _bdot(p, ckvb) * rl
        for j in range(hs):
            h = g * hs + j
            o_ref[:, off + h * V_B:off + (h + 1) * V_B] = _bdot(
                o_lat[j * tq:(j + 1) * tq], wuv_ref[:, h * V_B:(h + 1) * V_B]).astype(o_ref.dtype)


def prompt_mix(proj, iq_h, qlat_h, qpe_h, ckvn, ikkpe, tz, w_uv2, nb, seq, topk):
    tq = QBLK
    nq = seq // tq
    row = lambda w, c: pl.BlockSpec((tq, w), lambda b, q: (b * nq + q, c))
    heads = lambda a: pl.BlockSpec((a.shape[0], tq, a.shape[2]), lambda b, q: (0, b * nq + q, 0))
    keys = lambda w, c: pl.BlockSpec((seq, w), lambda b, q: (b, c))
    full = lambda a: pl.BlockSpec(a.shape, lambda b, q: (0,) * a.ndim)
    kern = functools.partial(_prompt_mix_kernel, topk=topk,
                             idx_bits=max(int(math.ceil(math.log2(seq))), 1))
    return pl.pallas_call(
        kern,
        grid=(nb, nq),
        in_specs=[row(1024, EV_QA // 1024), heads(iq_h), row(256, EV_IK // 256),
                  heads(qlat_h), heads(qpe_h),
                  keys(256, EV_KA // 256), keys(256, EV_VA // 256),
                  keys(KV_LORA, 0), keys(LANES, 0), full(tz), full(w_uv2)],
        out_specs=pl.BlockSpec((tq, 2048), lambda b, q: (b * nq + q, 0)),
        out_shape=jax.ShapeDtypeStruct((nb * seq, 2048), BF16),
        scratch_shapes=[pltpu.VMEM((HEAD_STACK * tq, seq), F32)],
        compiler_params=_cparams(("arbitrary", "arbitrary")),
        name="prompt_mix",
    )(proj, iq_h, proj, qlat_h, qpe_h, proj, proj, ckvn, ikkpe, tz, w_uv2)


def _page_copy(pt_ref, cache_ref, layer, b, j, buf, slot, sem, feature_major=False):
    src = cache_ref.at[layer, pt_ref[b, j]]
    if feature_major:
        dst = buf.at[:, pl.ds(j * PAGE_SIZE, PAGE_SIZE)]
    else:
        rows = cache_ref.shape[2]
        dst = buf.at[pl.ds(j * rows, rows)]
    return pltpu.make_async_copy(src, dst, sem.at[slot])


def _fetch_pages(step, n_steps, step_coords, copies):
    slot = step % 2

    @pl.when(step == 0)
    def _():
        for c in copies(step_coords(step), slot):
            c.start()

    @pl.when(step + 1 < n_steps)
    def _():
        for c in copies(step_coords(step + 1), 1 - slot):
            c.start()

    for c in copies(step_coords(step), slot):
        c.wait()
    return slot


SELECT_GROUP = 4


def _sample_select_kernel(pt_ref, q_ref, w_ref, new_ref, cache_ref, mask_ref, ibuf, s_scr, sem,
                          *, layer, n_pages, n_new, topk, idx_bits, chunk):
    ng = q_ref.shape[0]

    def copies(step, slot):
        return [_page_copy(pt_ref, cache_ref, layer, step * ng + g, j, ibuf.at[slot, g], slot, sem,
                           feature_major=True)
                for g in range(ng) for j in range(n_pages)]

    slot = _fetch_pages(pl.program_id(0), pl.num_programs(0), lambda s: s, copies)
    past = n_pages * PAGE_SIZE
    n = past + PAGE_SIZE

    def scores(qb, wq, keys_t):
        s = jnp.maximum(_bdot(qb, keys_t), 0.0) * wq
        acc = s[0:SUBLANES]
        for h in range(1, H_I):
            acc = acc + s[h * SUBLANES:(h + 1) * SUBLANES]
        return acc

    for g in range(ng):
        qb = q_ref[g].astype(BF16)
        wq = w_ref[g]
        rows = slice(g * SUBLANES, (g + 1) * SUBLANES)
        for c0 in range(0, past, chunk):
            s_scr[rows, c0:c0 + chunk] = scores(qb, wq, ibuf[slot, g, :, c0:c0 + chunk])
        s_scr[rows, past:] = scores(qb, wq, new_ref[g])
    tok = lax.broadcasted_iota(I32, (ng * SUBLANES, n), 0) % SUBLANES
    kpos = lax.broadcasted_iota(I32, (ng * SUBLANES, n), 1)
    valid = (kpos <= past + tok) & (kpos < past + n_new)
    sel = _topk_mask(s_scr[...], valid, kpos, topk, idx_bits) & valid
    mask_ref[...] = sel.astype(F32).reshape(ng, SUBLANES, n)


def sample_select(page_table, q_idx, w_idx, cache_idx, layer, ik_new_pad, n_new, topk):
    nb, n_pages = page_table.shape
    past = n_pages * PAGE_SIZE
    n = past + PAGE_SIZE
    kern = functools.partial(_sample_select_kernel, layer=layer, n_pages=n_pages, n_new=n_new,
                             topk=topk, idx_bits=int(math.ceil(math.log2(n))),
                             chunk=min(past, 1024))
    ng = math.gcd(nb, SELECT_GROUP)
    grid_spec = pltpu.PrefetchScalarGridSpec(
        num_scalar_prefetch=1,
        grid=(nb // ng,),
        in_specs=[pl.BlockSpec((ng, H_I * SUBLANES, D_I), lambda b, pt: (b, 0, 0)),
                  pl.BlockSpec((ng, H_I * SUBLANES, 1), lambda b, pt: (b, 0, 0)),
                  pl.BlockSpec((ng, D_I, PAGE_SIZE), lambda b, pt: (b, 0, 0)),
                  pl.BlockSpec(memory_space=pl.ANY)],
        out_specs=pl.BlockSpec((ng, SUBLANES, n), lambda b, pt: (b, 0, 0)),
        scratch_shapes=[pltpu.VMEM((2, ng, D_I, past), F32), pltpu.VMEM((ng * SUBLANES, n), F32),
                        pltpu.SemaphoreType.DMA((2,))],
    )
    return pl.pallas_call(
        kern, grid_spec=grid_spec,
        out_shape=jax.ShapeDtypeStruct((nb, SUBLANES, n), F32),
        compiler_params=_cparams(("arbitrary",)),
        name="sample_select",
    )(page_table, q_idx, w_idx, ik_new_pad, cache_idx)


def _sample_dsa_kernel(pt_ref, q_ref, mask_ref, tz_ref, knew_ref, vnew_ref, ck_ref, cv_ref, o_ref,
                       kbuf, vbuf, k_scr, v_scr, sem_k, sem_v, *, layer, n_pages):
    def copies(b, slot):
        out = []
        for j in range(n_pages):
            out.append(_page_copy(pt_ref, ck_ref, layer, b, j, kbuf.at[slot], slot, sem_k))
            out.append(_page_copy(pt_ref, cv_ref, layer, b, j, vbuf.at[slot], slot, sem_v))
        return out

    slot = _fetch_pages(pl.program_id(0), pl.num_programs(0), lambda s: s, copies)
    past = n_pages * PAGE_SIZE
    n = past + PAGE_SIZE
    grp = H_A // KV_A
    near = 2 * PAGE_SIZE
    m = mask_ref[...]
    sel = jnp.concatenate([m] * grp, axis=0) > 0.5
    for kv in range(KV_A):
        k_scr[:past, :] = kbuf[slot, pl.ds(kv, past, stride=KV_A), :].astype(BF16)
        v_scr[:past, :] = vbuf[slot, pl.ds(kv, past, stride=KV_A), :].astype(BF16)
        k_scr[past:, :] = knew_ref[:, kv * HD_A:(kv + 1) * HD_A].astype(BF16)
        v_scr[past:, :] = vnew_ref[:, kv * HD_A:(kv + 1) * HD_A].astype(BF16)
        s = _bdot_nt(q_ref[kv], k_scr[...]) * (HD_A ** -0.5)
        bias = jnp.concatenate([jnp.zeros((grp * SUBLANES, n - near), F32), tz_ref[kv]], axis=1)
        p, rl = _softmax_rows(jnp.where(sel, s + bias, NEG_INF))
        o_ref[kv] = _bdot(p, v_scr[...]) * rl


def sample_dsa(page_table, q_a, mask, tz_s, cache_k, cache_v, layer, k_new_pad, v_new_pad):
    nb, n_pages = page_table.shape
    past = n_pages * PAGE_SIZE
    n = past + PAGE_SIZE
    grp = H_A // KV_A
    rows = grp * SUBLANES
    kern = functools.partial(_sample_dsa_kernel, layer=layer, n_pages=n_pages)
    new_spec = pl.BlockSpec((None, PAGE_SIZE, KV_A * HD_A), lambda b, pt: (b, 0, 0))
    grid_spec = pltpu.PrefetchScalarGridSpec(
        num_scalar_prefetch=1,
        grid=(nb,),
        in_specs=[pl.BlockSpec((None, KV_A, rows, HD_A), lambda b, pt: (b, 0, 0, 0)),
                  pl.BlockSpec((None, SUBLANES, n), lambda b, pt: (b, 0, 0)),
                  pl.BlockSpec((KV_A, rows, 2 * PAGE_SIZE), lambda b, pt: (0, 0, 0)),
                  new_spec, new_spec,
                  pl.BlockSpec(memory_space=pl.ANY), pl.BlockSpec(memory_space=pl.ANY)],
        out_specs=pl.BlockSpec((None, KV_A, rows, HD_A), lambda b, pt: (b, 0, 0, 0)),
        scratch_shapes=[pltpu.VMEM((2, past * KV_A, HD_A), F32),
                        pltpu.VMEM((2, past * KV_A, HD_A), F32),
                        pltpu.VMEM((n, HD_A), BF16), pltpu.VMEM((n, HD_A), BF16),
                        pltpu.SemaphoreType.DMA((2,)), pltpu.SemaphoreType.DMA((2,))],
    )
    return pl.pallas_call(
        kern, grid_spec=grid_spec,
        out_shape=jax.ShapeDtypeStruct((nb, KV_A, rows, HD_A), F32),
        compiler_params=_cparams(("arbitrary",)),
        name="sample_dsa",
    )(page_table, q_a, mask, tz_s, k_new_pad, v_new_pad, cache_k, cache_v)


def _sample_mla_kernel(pt_ref, qlat_ref, qpe_ref, cnew_ref, rnew_ref, cc_ref, cr_ref, o_ref,
                       cbuf, rbuf, c_scr, r_scr, sem_c, sem_r, *, layer, n_pages, n_new):
    def copies(b, slot):
        out = []
        for j in range(n_pages):
            out.append(_page_copy(pt_ref, cc_ref, layer, b, j, cbuf.at[slot], slot, sem_c))
            out.append(_page_copy(pt_ref, cr_ref, layer, b, j, rbuf.at[slot], slot, sem_r,
                                  feature_major=True))
        return out

    slot = _fetch_pages(pl.program_id(0), pl.num_programs(0), lambda s: s, copies)
    past = n_pages * PAGE_SIZE
    n = past + PAGE_SIZE
    c_scr[:past, :] = cbuf[slot].astype(BF16)
    r_scr[:, :past] = rbuf[slot].astype(BF16)
    c_scr[past:, :] = cnew_ref[...].astype(BF16)
    r_scr[:, past:] = rnew_ref[...].astype(BF16)
    rows = H_B * SUBLANES
    s = (_bdot_nt(qlat_ref[...], c_scr[...]) + _bdot(qpe_ref[...], r_scr[...])) * (
        (NOPE_B + ROPE_B) ** -0.5)
    tok = lax.broadcasted_iota(I32, (rows, n), 0) % SUBLANES
    kpos = lax.broadcasted_iota(I32, (rows, n), 1)
    valid = (kpos <= past + tok) & (kpos < past + n_new)
    p, rl = _softmax_rows(jnp.where(valid, s, NEG_INF))
    o_ref[...] = _bdot(p, c_scr[...]) * rl


def sample_mla(page_table, q_lat, q_pe, cache_lat, cache_rope, layer, c_new_pad, r_new_pad, n_new):
    nb, n_pages = page_table.shape
    past = n_pages * PAGE_SIZE
    n = past + PAGE_SIZE
    rows = H_B * SUBLANES
    kern = functools.partial(_sample_mla_kernel, layer=layer, n_pages=n_pages, n_new=n_new)
    grid_spec = pltpu.PrefetchScalarGridSpec(
        num_scalar_prefetch=1,
        grid=(nb,),
        in_specs=[pl.BlockSpec((None, rows, KV_LORA), lambda b, pt: (b, 0, 0)),
                  pl.BlockSpec((None, rows, ROPE_B), lambda b, pt: (b, 0, 0)),
                  pl.BlockSpec((None, PAGE_SIZE, KV_LORA), lambda b, pt: (b, 0, 0)),
                  pl.BlockSpec((None, ROPE_B, PAGE_SIZE), lambda b, pt: (b, 0, 0)),
                  pl.BlockSpec(memory_space=pl.ANY), pl.BlockSpec(memory_space=pl.ANY)],
        out_specs=pl.BlockSpec((None, rows, KV_LORA), lambda b, pt: (b, 0, 0)),
        scratch_shapes=[pltpu.VMEM((2, past, KV_LORA), F32), pltpu.VMEM((2, ROPE_B, past), F32),
                        pltpu.VMEM((n, KV_LORA), BF16), pltpu.VMEM((ROPE_B, n), BF16),
                        pltpu.SemaphoreType.DMA((2,)), pltpu.SemaphoreType.DMA((2,))],
    )
    return pl.pallas_call(
        kern, grid_spec=grid_spec,
        out_shape=jax.ShapeDtypeStruct((nb, rows, KV_LORA), F32),
        compiler_params=_cparams(("arbitrary",)),
        name="sample_mla",
    )(page_table, q_lat, q_pe, c_new_pad, r_new_pad, cache_lat, cache_rope)


def _head_proj_kernel(o_ref, w_ref, y_ref):
    y_ref[...] = _bdot(o_ref[...], w_ref[...])


def head_proj(o_lat, w_uv_h):
    nh, m, c = o_lat.shape
    v = w_uv_h.shape[2]
    return pl.pallas_call(
        _head_proj_kernel,
        grid=(nh,),
        in_specs=[pl.BlockSpec((None, m, c), lambda h: (h, 0, 0)),
                  pl.BlockSpec((None, c, v), lambda h: (h, 0, 0))],
        out_specs=pl.BlockSpec((None, m, v), lambda h: (h, 0, 0)),
        out_shape=jax.ShapeDtypeStruct((nh, m, v), F32),
        compiler_params=_cparams(("arbitrary",)),
        name="head_proj",
    )(o_lat, w_uv_h)


def _rope_half(x, cos, sin):
    half = x.shape[-1] // 2
    x1, x2 = x[:, :half], x[:, half:]
    return jnp.concatenate([x1 * cos - x2 * sin, x1 * sin + x2 * cos], axis=-1)


def _group_norm_gate(o, g, gain):
    mu = jnp.mean(o, axis=-1, keepdims=True)
    d = o - mu
    var = jnp.mean(d * d, axis=-1, keepdims=True)
    on = (d * lax.rsqrt(var + EPS)) * gain
    return (g * jax.nn.sigmoid(g)) * on


def _ret_prompt_kernel(q_ref, k_ref, v_ref, g_ref, cos_ref, sin_ref, dmask_ref, cross_ref,
                       kdec_ref, cdec_ref, gain_ref, og_ref, st_ref, s_scr):
    c = pl.program_id(2)

    @pl.when(c == 0)
    def _():
        s_scr[...] = jnp.zeros_like(s_scr)

    cos, sin = cos_ref[...], sin_ref[...]
    q = _rope_half(q_ref[...], cos, sin)
    k = _rope_half(k_ref[...], cos, sin) * (DK_C ** -0.5)
    v = v_ref[...]
    s_old = s_scr[...]
    a = _bdot_nt(q, k) * dmask_ref[...]
    o = _bdot(a, v) + _bdot(q, s_old) * cross_ref[...]
    s_new = s_old * cdec_ref[...] + _bdot_tn(k * kdec_ref[...], v)
    s_scr[...] = s_new
    og_ref[...] = _group_norm_gate(o, g_ref[...], gain_ref[...]).astype(og_ref.dtype)

    @pl.when(c == pl.num_programs(2) - 1)
    def _():
        st_ref[...] = s_new


def ret_prompt(proj, tabs, gain, nb, seq):
    cos, sin, dmask, cross, kdec, cdec = tabs
    ch = dmask.shape[1]
    nc = seq // ch
    hq = H_C * DK_C
    kern = _ret_prompt_kernel
    per_head = lambda a: pl.BlockSpec((None,) + a.shape[1:], lambda b, h, c: (h,) + (0,) * (a.ndim - 1))
    return pl.pallas_call(
        kern,
        grid=(nb, H_C, nc),
        in_specs=[pl.BlockSpec((ch, DK_C), lambda b, h, c: (b * nc + c, h)),
                  pl.BlockSpec((ch, DK_C), lambda b, h, c: (b * nc + c, hq // DK_C + h)),
                  pl.BlockSpec((ch, DV_C), lambda b, h, c: (b * nc + c, 2 * hq // DV_C + h)),
                  pl.BlockSpec((ch, DV_C), lambda b, h, c: (b * nc + c, 2 * hq // DV_C + H_C + h)),
                  pl.BlockSpec((ch, DK_C // 2), lambda b, h, c: (c, 0)),
                  pl.BlockSpec((ch, DK_C // 2), lambda b, h, c: (c, 0)),
                  per_head(dmask), per_head(cross), per_head(kdec), per_head(cdec),
                  pl.BlockSpec((1, DV_C), lambda b, h, c: (0, h))],
        out_specs=[pl.BlockSpec((ch, DV_C), lambda b, h, c: (b * nc + c, h)),
                   pl.BlockSpec((None, None, DK_C, DV_C), lambda b, h, c: (b, h, 0, 0))],
        out_shape=[jax.ShapeDtypeStruct((nb * seq, H_C * DV_C), BF16),
                   jax.ShapeDtypeStruct((nb, H_C, DK_C, DV_C), F32)],
        scratch_shapes=[pltpu.VMEM((DK_C, DV_C), F32)],
        compiler_params=_cparams(("arbitrary", "arbitrary", "arbitrary")),
        name="ret_prompt",
    )(proj, proj, proj, proj, cos, sin, dmask, cross, kdec, cdec, gain)


def _ret_sample_kernel(p_ref, s0_ref, cos_ref, sin_ref, dmask_ref, cross_ref, kdec_ref, cdec_ref,
                       gain_ref, *refs):
    og_ref, st_ref = refs[-2:]
    cos, sin = cos_ref[...], sin_ref[...]
    hq = H_C * DK_C
    for h in range(H_C):
        q = _rope_half(p_ref[:, h * DK_C:(h + 1) * DK_C], cos, sin)
        k = _rope_half(p_ref[:, hq + h * DK_C:hq + (h + 1) * DK_C], cos, sin) * (DK_C ** -0.5)
        v = p_ref[:, 2 * hq + h * DV_C:2 * hq + (h + 1) * DV_C]
        g = p_ref[:, 2 * hq + (H_C + h) * DV_C:2 * hq + (H_C + h + 1) * DV_C]
        s_old = s0_ref[h]
        a = _bdot_nt(q, k) * dmask_ref[h]
        o = _bdot(a, v) + _bdot(q, s_old) * cross_ref[h]
        st_ref[h] = s_old * cdec_ref[h] + _bdot_tn(k * kdec_ref[h], v)
        og_ref[:, h * DV_C:(h + 1) * DV_C] = _group_norm_gate(
            o, g, gain_ref[:, h * DV_C:(h + 1) * DV_C]).astype(og_ref.dtype)


def ret_sample(proj_pad, state_c, layer, tabs, gain, st_all):
    cos, sin, dmask, cross, kdec, cdec = tabs
    nb = proj_pad.shape[0]
    full = lambda a: pl.BlockSpec(a.shape, lambda b: (0,) * a.ndim)
    ins = [proj_pad, state_c, cos, sin, dmask, cross, kdec, cdec, gain]
    in_specs = [pl.BlockSpec((None, SUBLANES, proj_pad.shape[2]), lambda b: (b, 0, 0)),
                pl.BlockSpec((None, None, H_C, DK_C, DV_C), lambda b: (layer, b, 0, 0, 0)),
                full(cos), full(sin), full(dmask), full(cross), full(kdec), full(cdec), full(gain)]
    aliases = {}
    if st_all is not None:
        ins.append(st_all)
        in_specs.append(pl.BlockSpec(memory_space=pl.ANY))
        aliases = {len(ins) - 1: 1}
    return pl.pallas_call(
        _ret_sample_kernel,
        grid=(nb,),
        in_specs=in_specs,
        out_specs=[pl.BlockSpec((None, SUBLANES, H_C * DV_C), lambda b: (b, 0, 0)),
                   pl.BlockSpec((None, None, H_C, DK_C, DV_C), lambda b: (layer, b, 0, 0, 0))],
        out_shape=[jax.ShapeDtypeStruct((nb, SUBLANES, H_C * DV_C), BF16),
                   jax.ShapeDtypeStruct(state_c.shape, F32)],
        input_output_aliases=aliases,
        compiler_params=_cparams(("arbitrary",)),
        name="ret_sample",
    )(*ins)


ROUTE_LANE0 = N_GROUPS


def _route_kernel(x_ref, g_ref, sh_ref, sc_ref, wr_ref, br_ref, cin_ref,
                  h_ref, mi_ref, mf_ref, cnt_ref, carry_scr):
    i = pl.program_id(0)

    @pl.when(i == 0)
    def _():
        carry_scr[...] = cin_ref[...]

    h = (_rms(x_ref[...]) * g_ref[...]) * (1.0 + sc_ref[...]) + sh_ref[...]
    h_ref[...] = h
    logits = jnp.dot(h, wr_ref[...], preferred_element_type=F32,
                     precision=lax.Precision.HIGHEST) + br_ref[...]
    tm = h.shape[0]
    lane = lax.broadcasted_iota(I32, (tm, LANES), 1)
    big = jnp.int32(LANES)

    def first_max(v):
        m = jnp.max(v, axis=1, keepdims=True)
        return m, jnp.min(jnp.where(v == m, lane, big), axis=1, keepdims=True)

    is_g = lane < N_GROUPS
    mg, gsel = first_max(jnp.where(is_g, logits, NEG_INF))
    gate_g = 1.0 / jnp.sum(jnp.where(is_g, jnp.exp(logits - mg), 0.0), axis=1, keepdims=True)
    lo = ROUTE_LANE0 + gsel * E_PER_GROUP
    le = jnp.where((lane >= lo) & (lane < lo + E_PER_GROUP), logits, NEG_INF)
    m1, i1 = first_max(le)
    m2, i2 = first_max(jnp.where(lane == i1, NEG_INF, le))
    e2 = jnp.exp(m2 - m1)
    w1 = gate_g / (1.0 + e2)
    w2 = gate_g * e2 / (1.0 + e2)

    hit1 = lane == i1
    hit2 = lane == i2
    oh = (hit1 | hit2).astype(BF16)
    r = lax.broadcasted_iota(I32, (tm, tm), 0)
    c = lax.broadcasted_iota(I32, (tm, tm), 1)
    cum = jnp.dot((c < r).astype(BF16), oh, preferred_element_type=F32) + carry_scr[0:1, :]
    rank1 = jnp.sum(jnp.where(hit1, cum, 0.0), axis=1, keepdims=True).astype(I32)
    rank2 = jnp.sum(jnp.where(hit2, cum, 0.0), axis=1, keepdims=True).astype(I32)
    carry_scr[...] = carry_scr[...] + jnp.sum(oh.astype(F32), axis=0, keepdims=True)
    cnt_ref[...] = carry_scr[...]

    mi_ref[...] = jnp.where(lane == 0, i1 - ROUTE_LANE0,
                            jnp.where(lane == 1, i2 - ROUTE_LANE0,
                                      jnp.where(lane == 2, rank1, jnp.where(lane == 3, rank2, 0))))
    mf_ref[...] = jnp.where(lane == 0, w1, jnp.where(lane == 1, w2, 0.0))


def moe_route(x, gain, shift, scale, w_router, b_router, carry_in, rows_per_group, tm=256):
    m, d = x.shape
    tm = min(tm, m)
    steps = max(rows_per_group // tm, 1)
    shift, scale = _per_block(shift, tm), _per_block(scale, tm)
    mod = lambda a: pl.BlockSpec((None,) + a.shape[1:], lambda i: (i // steps, 0, 0))
    return pl.pallas_call(
        _route_kernel,
        grid=(m // tm,),
        in_specs=[pl.BlockSpec((tm, d), lambda i: (i, 0)),
                  pl.BlockSpec((1, d), lambda i: (0, 0)),
                  mod(shift), mod(scale),
                  pl.BlockSpec((d, LANES), lambda i: (0, 0)),
                  pl.BlockSpec((1, LANES), lambda i: (0, 0)),
                  pl.BlockSpec((SUBLANES, LANES), lambda i: (0, 0))],
        out_specs=[pl.BlockSpec((tm, d), lambda i: (i, 0)),
                   pl.BlockSpec((tm, LANES), lambda i: (i, 0)),
                   pl.BlockSpec((tm, LANES), lambda i: (i, 0)),
                   pl.BlockSpec((SUBLANES, LANES), lambda i: (0, 0))],
        out_shape=[jax.ShapeDtypeStruct((m, d), F32),
                   jax.ShapeDtypeStruct((m, LANES), I32),
                   jax.ShapeDtypeStruct((m, LANES), F32),
                   jax.ShapeDtypeStruct((SUBLANES, LANES), F32)],
        scratch_shapes=[pltpu.VMEM((SUBLANES, LANES), F32)],
        compiler_params=_cparams(("arbitrary",)),
        name="moe_route",
    )(x, gain.reshape(1, d), shift, scale, w_router, b_router, carry_in)


def _expert_kernel(te_ref, tok_ref, h_ref, w1_ref, w3_ref, w2_ref, y_ref,
                   xbuf0, xbuf1, w1b, w3b, w2b, sem, *, tm):
    i = pl.program_id(0)
    n_used = te_ref[1, 0]
    bufs = (xbuf0, xbuf1)

    def row_copy(tile, r, p):
        return pltpu.make_async_copy(h_ref.at[pl.ds(tok_ref[tile * tm + r], 1)],
                                     bufs[p].at[pl.ds(r, 1)], sem.at[p])

    def loop_rows(fn):
        def body(r, carry):
            fn(r)
            return carry
        lax.fori_loop(0, tm, body, 0)

    @pl.when(i == 0)
    def _():
        loop_rows(lambda r: row_copy(0, r, 0).start())

    @pl.when((i < n_used) & (te_ref[2, i] > 0))
    def _():
        w1b[...] = w1_ref[...].astype(BF16)
        w3b[...] = w3_ref[...].astype(BF16)
        w2b[...] = w2_ref[...].astype(BF16)

    nxt = jnp.minimum(i + 1, n_used - 1)
    for p in range(2):
        @pl.when((i < n_used) & (i % 2 == p))
        def _():
            for r in range(tm):
                row_copy(i, r, p).wait()
            for r in range(tm):
                row_copy(nxt, r, 1 - p).start()
            x = bufs[p][...].astype(BF16)
            a = jnp.dot(x, w1b[...], preferred_element_type=F32)
            b = jnp.dot(x, w3b[...], preferred_element_type=F32)
            act = (a * jax.nn.sigmoid(a)) * b
            y_ref[...] = jnp.dot(act.astype(BF16), w2b[...], preferred_element_type=F32)

        @pl.when((i == n_used) & (i % 2 == p))
        def _():
            loop_rows(lambda r: row_copy(n_used - 1, r, p).wait())

    @pl.when(i >= n_used)
    def _():
        y_ref[...] = jnp.zeros_like(y_ref)


def moe_experts(tile_info, tok_of_slot, h, w1, w3, w2, layer, tm):
    n_slots = tok_of_slot.shape[0]
    d = h.shape[1]
    f = w1.shape[3]
    wspec = lambda r, c: pl.BlockSpec((None, None, r, c), lambda i, te, tok: (layer, te[0, i], 0, 0))
    grid_spec = pltpu.PrefetchScalarGridSpec(
        num_scalar_prefetch=2,
        grid=(n_slots // tm,),
        in_specs=[pl.BlockSpec(memory_space=pl.ANY), wspec(d, f), wspec(d, f), wspec(f, d)],
        out_specs=pl.BlockSpec((tm, d), lambda i, te, tok: (i, 0)),
        scratch_shapes=[pltpu.VMEM((tm, d), F32), pltpu.VMEM((tm, d), F32), pltpu.VMEM((d, f), BF16),
                        pltpu.VMEM((d, f), BF16), pltpu.VMEM((f, d), BF16),
                        pltpu.SemaphoreType.DMA((2,))],
    )
    return pl.pallas_call(
        functools.partial(_expert_kernel, tm=tm), grid_spec=grid_spec,
        out_shape=jax.ShapeDtypeStruct((n_slots, d), F32),
        compiler_params=_cparams(("arbitrary",)),
        name="moe_experts",
    )(tile_info, tok_of_slot, h, w1, w3, w2)


def _combine_kernel(pos_ref, x_ref, gate_ref, mf_ref, ys_ref, o_ref, buf, sem, *, tb, tok0):
    base = tok0 + pl.program_id(0) * tb

    def issue(t, carry):
        for j in range(2):
            pltpu.make_async_copy(ys_ref.at[pl.ds(pos_ref[2 * (base + t) + j], 1)],
                                  buf.at[j, pl.ds(t, 1)], sem).start()
        return carry

    lax.fori_loop(0, tb, issue, 0)

    def drain(t, carry):
        for j in range(2):
            pltpu.make_async_copy(ys_ref.at[pl.ds(0, 1)], buf.at[j, pl.ds(0, 1)], sem).wait()
        return carry

    lax.fori_loop(0, tb, drain, 0)
    y = mf_ref[:, 0:1] * buf[0] + mf_ref[:, 1:2] * buf[1]
    o_ref[...] = x_ref[...] + gate_ref[...] * y


def moe_combine(pos_flat, x, gate, mf, ys, tok0, rows_per_group, tb=256):
    m, d = x.shape
    tb = min(tb, m)
    steps = max(rows_per_group // tb, 1)
    gate = _per_block(gate, tb)
    grid_spec = pltpu.PrefetchScalarGridSpec(
        num_scalar_prefetch=1,
        grid=(m // tb,),
        in_specs=[pl.BlockSpec((tb, d), lambda i, p: (i, 0)),
                  pl.BlockSpec((None,) + gate.shape[1:], lambda i, p: (i // steps, 0, 0)),
                  pl.BlockSpec((tb, LANES), lambda i, p: (i, 0)),
                  pl.BlockSpec(memory_space=pl.ANY)],
        out_specs=pl.BlockSpec((tb, d), lambda i, p: (i, 0)),
        scratch_shapes=[pltpu.VMEM((2, tb, d), F32), pltpu.SemaphoreType.DMA(())],
    )
    return pl.pallas_call(
        functools.partial(_combine_kernel, tb=tb, tok0=tok0), grid_spec=grid_spec,
        out_shape=jax.ShapeDtypeStruct((m, d), F32),
        compiler_params=_cparams(("arbitrary",)),
        name="moe_combine",
    )(pos_flat, x, gate, mf, ys)


def _final_norm_kernel(x_ref, g_ref, o_ref):
    o_ref[...] = _rms(x_ref[...]) * g_ref[...]


def final_norm(x, gain, tm=512):
    m, d = x.shape
    tm = min(tm, m)
    return pl.pallas_call(
        _final_norm_kernel,
        grid=(m // tm,),
        in_specs=[pl.BlockSpec((tm, d), lambda i: (i, 0)), pl.BlockSpec((1, d), lambda i: (0, 0))],
        out_specs=pl.BlockSpec((tm, d), lambda i: (i, 0)),
        out_shape=jax.ShapeDtypeStruct((m, d), F32),
        compiler_params=_cparams(("arbitrary",)),
        name="final_norm",
    )(x, gain.reshape(1, d))


def _t5_bucket(dist):
    max_exact = N_BUCKETS // 2
    n = jnp.maximum(dist, 0)
    large = max_exact + (jnp.log(jnp.maximum(n, 1).astype(F32) / max_exact)
                         / math.log(MAX_DISTANCE / max_exact) * (N_BUCKETS - max_exact)).astype(I32)
    large = jnp.minimum(large, N_BUCKETS - 1)
    return jnp.where(n < max_exact, n, large)


def _rope_cos_sin(pos, half):
    inv = ROPE_BASE ** (-jnp.arange(half, dtype=F32) / half)
    ang = pos.astype(F32)[:, None] * inv[None, :]
    return jnp.cos(ang), jnp.sin(ang)


def _even_rope_tables(pos):
    cos, sin = _rope_cos_sin(pos, ROPE_B // 2)
    c64 = jnp.concatenate([cos, cos], axis=1)
    s64 = jnp.concatenate([-sin, sin], axis=1)
    one = jnp.ones_like(c64)
    zero = jnp.zeros_like(s64)
    return (jnp.tile(c64, (1, H_B)), jnp.tile(s64, (1, H_B)),
            jnp.concatenate([one, c64], axis=1), jnp.concatenate([zero, s64], axis=1))


def _retention_tables(pos, c):
    log_g = jnp.log1p(-(2.0 ** (-5.0 - jnp.arange(H_C, dtype=F32))))
    i = jnp.arange(c, dtype=F32)
    diff = i[:, None] - i[None, :]
    dmask = jnp.where(diff >= 0, jnp.exp(jnp.maximum(diff, 0.0)[None] * log_g[:, None, None]), 0.0)
    cross = jnp.exp((i[:, None] + 1.0) * log_g[None, :])
    kdec = jnp.exp((c - 1.0 - i)[:, None] * log_g[None, :])
    chunk_dec = jnp.exp(c * log_g)
    cos, sin = _rope_cos_sin(pos, DK_C // 2)
    return dmask, cross.T[:, :, None], kdec.T[:, :, None], chunk_dec[:, None, None], cos, sin


def _pad_rows(a, axis, n):
    pad = [(0, 0)] * a.ndim
    pad[axis] = (0, n - a.shape[axis])
    return jnp.pad(a, pad)


def _even_weight_layout(w_in):
    qa, ka, va, iq, ik, iw, cq, ckv, kpe = jnp.split(
        w_in, np.cumsum([1024, 256, 256, 1024, 64, 16, 512, 256, 64])[:-1].tolist(), axis=1)
    pad = jnp.zeros((w_in.shape[0], EV_COLS - EV_IW - H_I), w_in.dtype)
    return jnp.concatenate([qa, iq, cq, ka, va, ckv, ik, kpe, iw, pad], axis=1)


def _moe_block(xp, xs, mods_p, mods_s, gain, w_router, b_router, w1, w3, w2, layer, seq, tm_e=256):
    sh_p, sc_p, gt_p = mods_p
    sh_s, sc_s, gt_s = mods_s
    n_p, n_s = xp.shape[0], xs.shape[0]
    zero_carry = jnp.zeros((SUBLANES, LANES), F32)
    h_p, mi_p, mf_p, cnt_p = moe_route(xp, gain, sh_p, sc_p, w_router, b_router, zero_carry, seq)
    h_s, mi_s, mf_s, cnt = moe_route(xs, gain, sh_s, sc_s, w_router, b_router, cnt_p, 1)
    h = jnp.concatenate([h_p, h_s], axis=0)
    mi = jnp.concatenate([mi_p, mi_s], axis=0)
    eid, rank = mi[:, 0:2], mi[:, 2:4]
    counts = cnt[0, ROUTE_LANE0:ROUTE_LANE0 + N_EXPERTS].astype(I32)
    padded = ((counts + tm_e - 1) // tm_e) * tm_e
    ends = jnp.cumsum(padded)
    off = ends - padded
    pos = (off[eid] + rank).reshape(-1).astype(I32)
    n_tiles = (2 * (n_p + n_s)) // tm_e + N_EXPERTS + 1
    starts = jnp.arange(n_tiles, dtype=I32) * tm_e
    tile_e = jnp.searchsorted(ends, starts, side="right").astype(I32)
    valid = (starts < ends[-1]).astype(I32)
    last_e = jnp.max(jnp.where(counts > 0, jnp.arange(N_EXPERTS, dtype=I32), 0))
    tile_e = jnp.where(valid > 0, jnp.minimum(tile_e, N_EXPERTS - 1), last_e)
    changed = jnp.concatenate([jnp.ones((1,), I32), (tile_e[1:] != tile_e[:-1]).astype(I32)])
    tile_info = jnp.stack([tile_e, jnp.broadcast_to(jnp.sum(valid), (n_tiles,)), changed])
    tok_of_slot = jnp.zeros((n_tiles * tm_e,), I32).at[pos].set(
        jnp.arange(pos.shape[0], dtype=I32) // 2)
    ys = moe_experts(tile_info, tok_of_slot, h, w1, w3, w2, layer, tm_e)
    xp = moe_combine(pos, xp, gt_p, mf_p, ys, 0, seq)
    xs = moe_combine(pos, xs, gt_s, mf_s, ys, n_p, 1)
    return xp, xs


def kernel(x_prompt, x_sample, cache_a_k, cache_a_v, cache_a_idx, cache_b_latent, cache_b_rope, state_c,
           page_table, c_prompt, c_sample, rel_bias, w_ada, b_ada, norm_mix, norm_ffn, norm_final,
           w_in_even, q_norm_b, kv_norm_b, w_uq_b, w_uk_b, w_uv_b, w_out_even, w_in_odd, gn_gain_c,
           w_out_odd, w_rg, b_rg, w_re, b_re, w1, w3, w2):
    nb, seq, d = x_prompt.shape
    db, ds, _ = x_sample.shape
    n_p, n_s = nb * seq, db * ds
    past = page_table.shape[1] * PAGE_SIZE
    topk_p = min(TOPK_MAX, seq // 4)
    topk_s = min(TOPK_MAX, (past + ds) // 4)
    pos_p = jnp.arange(seq, dtype=I32)
    pos_s = past + jnp.arange(ds, dtype=I32)

    n_seq = nb + db
    r_pad = -(-n_seq // SUBLANES) * SUBLANES
    c_all = _pad_rows(jnp.concatenate([c_prompt, c_sample], axis=0), 0, r_pad)
    ada = ada_all(c_all, w_ada, b_ada)

    def mods(l):
        mp = ada[l, :nb].reshape(nb, 6, 1, d)
        ms = ada[l, nb:n_seq].reshape(db, 6, d)
        ms = jnp.repeat(ms[:, :, None, :], ds, axis=2)
        ms = jnp.moveaxis(ms, 1, 0).reshape(6, 1, n_s, d)
        return [mp[:, j] for j in range(6)], [ms[j] for j in range(6)]

    rb = rel_bias - rel_bias[N_BUCKETS - 1][None, :]
    ii = jnp.arange(QBLK, dtype=I32)[:, None]
    cc = jnp.arange(2 * QBLK, dtype=I32)[None, :]
    tz_p = jnp.moveaxis(rb[_t5_bucket(ii - cc + QBLK)], -1, 0)
    tt = jnp.minimum(jnp.arange(SUBLANES, dtype=I32), ds - 1)[:, None]
    near0 = past + PAGE_SIZE - 2 * PAGE_SIZE
    tz_s = jnp.moveaxis(rb[_t5_bucket(past + tt - (near0 + cc))], -1, 0)
    grp = H_A // KV_A
    tz_s = tz_s.reshape(KV_A, grp * SUBLANES, 2 * PAGE_SIZE)

    ev_tabs_p = _even_rope_tables(pos_p)
    ev_tabs_s = tuple(jnp.tile(t, (db, 1)) for t in _even_rope_tables(pos_s))
    ch_p = min(RET_CHUNK, seq)
    dmask_p, cross_p, kdec_p, cdec_p, cos_p, sin_p = _retention_tables(pos_p, ch_p)
    ret_tabs_p = (cos_p, sin_p, dmask_p, cross_p, kdec_p, cdec_p)
    dmask_s, cross_s, kdec_s, cdec_s, cos_s, sin_s = _retention_tables(pos_s, ds)
    ret_tabs_s = (_pad_rows(cos_s, 0, SUBLANES), _pad_rows(sin_s, 0, SUBLANES),
                  _pad_rows(_pad_rows(dmask_s, 1, SUBLANES), 2, SUBLANES),
                  _pad_rows(cross_s, 1, SUBLANES), _pad_rows(kdec_s, 1, SUBLANES), cdec_s)

    ck = cache_a_k.reshape(cache_a_k.shape[:2] + (PAGE_SIZE * KV_A, HD_A))
    cv = cache_a_v.reshape(cache_a_v.shape[:2] + (PAGE_SIZE * KV_A, HD_A))
    cidx_t = jnp.swapaxes(cache_a_idx, 2, 3)
    crope_t = jnp.swapaxes(cache_b_rope, 2, 3)

    xp = x_prompt.reshape(n_p, d)
    xs = x_sample.reshape(n_s, d)
    outs_p = {k: [] for k in ("k", "v", "idx", "lat", "rope", "st")}
    outs_s = {k: [] for k in ("k", "v", "idx", "lat", "rope")}
    st_s_all = None

    def head_tok_rows(a, heads, width):
        a = a.reshape(db, ds, heads, width).transpose(0, 2, 1, 3)
        return _pad_rows(a, 2, SUBLANES).reshape(db, heads * SUBLANES, width)

    def head_major_rows(a):
        heads, _, width = a.shape
        a = a.reshape(heads, db, ds, width).transpose(1, 0, 2, 3)
        return _pad_rows(a, 2, SUBLANES).reshape(db, heads * SUBLANES, width)

    def new_page(a):
        return _pad_rows(a.reshape(db, ds, a.shape[-1]), 1, PAGE_SIZE)

    for l in range(DEPTH):
        mp, ms = mods(l)
        if l % 2 == 0:
            e = l // 2
            w_in = _even_weight_layout(w_in_even[e])
            qn, kvn = q_norm_b[e].reshape(1, -1), kv_norm_b[e].reshape(1, -1)
            wuq = w_uq_b[e].reshape(Q_LORA, H_B, NOPE_B + ROPE_B)
            wuq_r = jnp.concatenate([wuq[:, :, :NOPE_B].reshape(Q_LORA, -1),
                                     wuq[:, :, NOPE_B:].reshape(Q_LORA, -1)], axis=1)
            wuk_t = jnp.transpose(w_uk_b[e], (1, 2, 0))
            wuv2 = w_uv_b[e].reshape(KV_LORA, H_B * V_B)
            wuv_h = jnp.transpose(w_uv_b[e], (1, 0, 2))

            proj = nm_matmul(xp, norm_mix[l], mp[0], mp[1], w_in, seq)
            qlat, qpe, ckvn, ikkpe, iq_h = even_post(proj, qn, kvn, wuq_r, wuk_t, ev_tabs_p)
            mix = prompt_mix(proj, iq_h, qlat, qpe, ckvn, ikkpe, tz_p, wuv2, nb, seq, topk_p)
            xp = mm_residual(mix, w_out_even[e], xp, mp[2], seq)
            outs_p["k"].append(proj[:, EV_KA:EV_KA + 256].reshape(nb, seq, KV_A, HD_A))
            outs_p["v"].append(proj[:, EV_VA:EV_VA + 256].reshape(nb, seq, KV_A, HD_A))
            outs_p["idx"].append(ikkpe[:, :D_I].reshape(nb, seq, D_I))
            outs_p["lat"].append(ckvn.reshape(nb, seq, KV_LORA))
            outs_p["rope"].append(ikkpe[:, D_I:].reshape(nb, seq, ROPE_B))

            proj_s = nm_matmul(xs, norm_mix[l], ms[0], ms[1], w_in, 1)
            qlat_s, qpe_s, ckvn_s, ikkpe_s, iq_s = even_post(proj_s, qn, kvn, wuq_r, wuk_t, ev_tabs_s)
            ka_s = proj_s[:, EV_KA:EV_KA + 256]
            va_s = proj_s[:, EV_VA:EV_VA + 256]
            ik_s, kpe_s = ikkpe_s[:, :D_I], ikkpe_s[:, D_I:]
            q_idx = head_major_rows(iq_s)
            w_idx = head_tok_rows(proj_s[:, EV_IW:EV_IW + H_I] * (H_I ** -0.5 * D_I ** -0.5), H_I, 1)
            mask = sample_select(page_table, q_idx, w_idx, cidx_t, e,
                                 jnp.swapaxes(new_page(ik_s), 1, 2), ds, topk_s)
            q_a = head_tok_rows(proj_s[:, EV_QA:EV_QA + H_A * HD_A], H_A, HD_A)
            q_a = q_a.reshape(db, KV_A, grp * SUBLANES, HD_A)
            oa = sample_dsa(page_table, q_a, mask, tz_s, ck, cv, e, new_page(ka_s), new_page(va_s))
            oa = oa.reshape(db, H_A, SUBLANES, HD_A)[:, :, :ds].transpose(0, 2, 1, 3).reshape(n_s, -1)
            o_lat = sample_mla(page_table, head_major_rows(qlat_s),
                               head_major_rows(qpe_s), cache_b_latent, crope_t, e,
                               new_page(ckvn_s), jnp.swapaxes(new_page(kpe_s), 1, 2), ds)
            o_lat = o_lat.reshape(db, H_B, SUBLANES, KV_LORA)[:, :, :ds]
            o_lat = o_lat.transpose(1, 0, 2, 3).reshape(H_B, n_s, KV_LORA)
            ob = head_proj(o_lat, wuv_h).transpose(1, 0, 2).reshape(n_s, -1)
            mix_s = jnp.concatenate([oa, ob], axis=1)
            xs = mm_residual(mix_s, w_out_even[e], xs, ms[2], 1)
            outs_s["k"].append(ka_s.reshape(db, ds, KV_A, HD_A))
            outs_s["v"].append(va_s.reshape(db, ds, KV_A, HD_A))
            outs_s["idx"].append(ik_s.reshape(db, ds, D_I))
            outs_s["lat"].append(ckvn_s.reshape(db, ds, KV_LORA))
            outs_s["rope"].append(kpe_s.reshape(db, ds, ROPE_B))
        else:
            o = l // 2
            gain = gn_gain_c[o].reshape(1, -1)
            proj = nm_matmul(xp, norm_mix[l], mp[0], mp[1], w_in_odd[o], seq)
            og, st_p = ret_prompt(proj, ret_tabs_p, gain, nb, seq)
            xp = mm_residual(og, w_out_odd[o], xp, mp[2], seq)
            outs_p["st"].append(st_p)
            proj_s = nm_matmul(xs, norm_mix[l], ms[0], ms[1], w_in_odd[o], 1)
            proj_s = _pad_rows(proj_s.reshape(db, ds, -1), 1, SUBLANES)
            og_s, st_s_all = ret_sample(proj_s, state_c, o, ret_tabs_s, gain, st_s_all)
            og_s = og_s[:, :ds].reshape(n_s, -1)
            xs = mm_residual(og_s, w_out_odd[o], xs, ms[2], 1)

        w_router = jnp.concatenate(
            [w_rg[l], w_re[l], jnp.zeros((d, LANES - N_GROUPS - N_EXPERTS), F32)], axis=1)
        b_router = jnp.concatenate(
            [b_rg[l], b_re[l], jnp.zeros((LANES - N_GROUPS - N_EXPERTS,), F32)]).reshape(1, LANES)
        xp, xs = _moe_block(xp, xs, (mp[3], mp[4], mp[5]), (ms[3], ms[4], ms[5]), norm_ffn[l],
                            w_router, b_router, w1, w3, w2, l, seq)

    y_p = final_norm(xp, norm_final).reshape(nb, seq, d)
    y_s = final_norm(xs, norm_final).reshape(db, ds, d)
    return (y_p, y_s,
            jnp.stack(outs_p["k"]), jnp.stack(outs_p["v"]), jnp.stack(outs_p["idx"]),
            jnp.stack(outs_p["lat"]), jnp.stack(outs_p["rope"]), jnp.stack(outs_p["st"]),
            jnp.stack(outs_s["k"]), jnp.stack(outs_s["v"]), jnp.stack(outs_s["idx"]),
            jnp.stack(outs_s["lat"]), jnp.stack(outs_s["rope"]), st_s_all)
```
